```python
import math
import jax, jax.numpy as jnp
from jax import lax
import numpy as np

D_MODEL = 1024
BATCH = 4
SEQ = 4096
DEPTH = 1
DEC_BATCH = 16
DEC_SEQ = 2048
PAST_LEN = 128

HEAD_DIM = 64
DIL_PATTERNS = ((128, 1), (512, 4), (2048, 16))
N_DIL_GROUPS = len(DIL_PATTERNS)
A_HEADS_PER_GROUP = 4
A_HEADS = N_DIL_GROUPS * A_HEADS_PER_GROUP
A_WIDTH = A_HEADS * HEAD_DIM
A_OUT_WIDTH = A_HEADS_PER_GROUP * HEAD_DIM
B_Q_HEADS = 8
B_KV_HEADS = 2
B_GROUP = B_Q_HEADS // B_KV_HEADS
B_Q_WIDTH = B_Q_HEADS * HEAD_DIM
B_KV_WIDTH = B_KV_HEADS * HEAD_DIM
B_RADIUS = 128
BLOCK = 128
N_HEADS_TOTAL = A_HEADS + B_Q_HEADS
N_BUCKETS = 32
MAX_DISTANCE = 1024
N_BRANCH = 2
IN_WIDTH = 3 * A_WIDTH + B_Q_WIDTH + 2 * B_KV_WIDTH + N_BRANCH * D_MODEL
D_FF = 2816
CONV_WIDTH = 3
EPS = 1e-6
NEG = -1e30

kernel_name = 'hybrid_dilated_window_encoder'


def rmsnorm(x, g):
    xf = x.astype(jnp.float32)
    y = xf * lax.rsqrt(jnp.mean(xf * xf, axis=-1, keepdims=True) + EPS)
    return (y * g.astype(jnp.float32)).astype(x.dtype)


def rel_bucket(rel):
    nb = N_BUCKETS // 2
    max_exact = nb // 2
    ret = jnp.where(rel > 0, nb, 0)
    n = jnp.abs(rel)
    nf = jnp.maximum(n, 1).astype(jnp.float32)
    large = max_exact + (jnp.log(nf / max_exact) / math.log(MAX_DISTANCE / max_exact)
                         * (nb - max_exact)).astype(jnp.int32)
    large = jnp.minimum(large, nb - 1)
    return ret + jnp.where(n < max_exact, n, large)


def dilated_group(q, k, v, bias, window, dilation):
    b, s, h, dh = q.shape
    n_side = window // (2 * dilation)
    pad = n_side * dilation
    offsets = jnp.arange(-n_side, n_side + 1, dtype=jnp.int32) * dilation
    kp = jnp.pad(k, ((0, 0), (pad, pad), (0, 0), (0, 0)))
    vp = jnp.pad(v, ((0, 0), (pad, pad), (0, 0), (0, 0)))
    scale = HEAD_DIM ** -0.5
    bias_f = bias.astype(jnp.float32)

    def block(i0):
        qi = lax.dynamic_slice_in_dim(q, i0, BLOCK, axis=1)
        pos = i0 + jnp.arange(BLOCK, dtype=jnp.int32)
        idx = pos[:, None] + offsets[None, :]
        valid = (idx >= 0) & (idx < s)
        kg = kp[:, idx + pad]
        vg = vp[:, idx + pad]
        logits = jnp.einsum('bthd,btkhd->bhtk', qi, kg).astype(jnp.float32) * scale
        logits = jnp.where(valid[None, None], logits + bias_f[None, :, None, :], NEG)
        lse = jax.nn.logsumexp(logits, axis=-1)
        p = jnp.exp(logits - lse[..., None])
        o = jnp.einsum('bhtk,btkhd->bthd', p.astype(v.dtype), vg)
        return o, lse

    starts = jnp.arange(s // BLOCK, dtype=jnp.int32) * BLOCK
    o, lse = lax.map(block, starts)
    o = o.transpose(1, 0, 2, 3, 4).reshape(b, s, h, dh)
    lse = lse.transpose(1, 2, 0, 3).reshape(b, h, s)
    return o, lse


def dilated_mixture(q, k, v, rel_bias):
    outs, lses = [], []
    for gi, (window, dilation) in enumerate(DIL_PATTERNS):
        hs = slice(gi * A_HEADS_PER_GROUP, (gi + 1) * A_HEADS_PER_GROUP)
        n_side = window // (2 * dilation)
        offs = jnp.arange(-n_side, n_side + 1, dtype=jnp.int32) * dilation
        bias = rel_bias[rel_bucket(offs)][:, hs].T
        o, lse = dilated_group(q[:, :, hs], k[:, :, hs], v[:, :, hs], bias, window, dilation)
        outs.append(o)
        lses.append(lse)
    w = jax.nn.softmax(jnp.stack(lses, 0), axis=0)
    out = jnp.einsum('gbhs,gbshd->bshd', w, jnp.stack(outs, 0).astype(jnp.float32))
    return out.astype(q.dtype)


def window_gqa(q, k, v, rel_bias, sink):
    b, s, hq, dh = q.shape
    span = BLOCK + 2 * B_RADIUS
    kp = jnp.pad(k, ((0, 0), (B_RADIUS, B_RADIUS), (0, 0), (0, 0)))
    vp = jnp.pad(v, ((0, 0), (B_RADIUS, B_RADIUS), (0, 0), (0, 0)))
    qg = q.reshape(b, s, B_KV_HEADS, B_GROUP, dh)
    rel = (jnp.arange(span, dtype=jnp.int32)[None, :] - B_RADIUS) - jnp.arange(BLOCK, dtype=jnp.int32)[:, None]
    band = jnp.abs(rel) <= B_RADIUS
    bias = rel_bias[rel_bucket(rel)][..., A_HEADS:]
    bias = bias.transpose(2, 0, 1).reshape(B_KV_HEADS, B_GROUP, BLOCK, span).astype(jnp.float32)
    sink_f = sink.astype(jnp.float32).reshape(B_KV_HEADS, B_GROUP)[None, :, :, None, None]
    scale = HEAD_DIM ** -0.5

    def block(i0):
        qi = lax.dynamic_slice_in_dim(qg, i0, BLOCK, axis=1)
        ki = lax.dynamic_slice_in_dim(kp, i0, span, axis=1)
        vi = lax.dynamic_slice_in_dim(vp, i0, span, axis=1)
        kpos = i0 - B_RADIUS + jnp.arange(span, dtype=jnp.int32)
        valid = band & ((kpos >= 0) & (kpos < s))[None, :]
        logits = jnp.einsum('btcgd,bucd->bcgtu', qi, ki).astype(jnp.float32) * scale
        logits = jnp.where(valid, logits + bias[None], NEG)
        m = jnp.maximum(jnp.max(logits, axis=-1, keepdims=True), sink_f)
        e = jnp.exp(logits - m)
        p = e / (jnp.sum(e, axis=-1, keepdims=True) + jnp.exp(sink_f - m))
        return jnp.einsum('bcgtu,bucd->btcgd', p.astype(v.dtype), vi)

    starts = jnp.arange(s // BLOCK, dtype=jnp.int32) * BLOCK
    o = lax.map(block, starts)
    return o.transpose(1, 0, 2, 3, 4, 5).reshape(b, s, hq * dh)


def dwconv(u, w, bias):
    c = u.shape[-1]
    y = lax.conv_general_dilated(u, w[:, None, :].astype(u.dtype), window_strides=(1,),
                                 padding=((CONV_WIDTH // 2, CONV_WIDTH // 2),),
                                 dimension_numbers=('NWC', 'WIO', 'NWC'),
                                 feature_group_count=c)
    return y + bias


def encoder_layer(x, g_attn, w_in, b_gate, rel_bias, sink, w_a_out, w_b_out, w_o,
                  g_ffn, w_up, conv_w, conv_b, w_down):
    b, s, _ = x.shape
    h = rmsnorm(x, g_attn)
    proj = h @ w_in
    cuts = np.cumsum([A_WIDTH, A_WIDTH, A_WIDTH, B_Q_WIDTH, B_KV_WIDTH, B_KV_WIDTH]).tolist()
    qa, ka, va, qb, kb, vb, gates = jnp.split(proj, cuts, axis=-1)
    a = dilated_mixture(qa.reshape(b, s, A_HEADS, HEAD_DIM), ka.reshape(b, s, A_HEADS, HEAD_DIM),
                        va.reshape(b, s, A_HEADS, HEAD_DIM), rel_bias)
    a = a.reshape(b, s, A_OUT_WIDTH) @ w_a_out
    bo = window_gqa(qb.reshape(b, s, B_Q_HEADS, HEAD_DIM), kb.reshape(b, s, B_KV_HEADS, HEAD_DIM),
                    vb.reshape(b, s, B_KV_HEADS, HEAD_DIM), rel_bias, sink)
    bo = bo @ w_b_out
    g_a, g_b = jnp.split(jax.nn.sigmoid(gates + b_gate), 2, axis=-1)
    x = x + (g_a * a + g_b * bo) @ w_o
    u = dwconv(rmsnorm(x, g_ffn) @ w_up, conv_w, conv_b)
    u_gate, u_val = jnp.split(u, 2, axis=-1)
    return x + (jax.nn.gelu(u_gate) * u_val) @ w_down


def trunk(x, g_attn, w_in, b_gate, rel_bias, sink, w_a_out, w_b_out, w_o,
          g_ffn, w_up, conv_w, conv_b, w_down, g_final):
    for l in range(DEPTH):
        x = encoder_layer(x, g_attn[l], w_in[l], b_gate[l], rel_bias, sink[l], w_a_out[l],
                          w_b_out[l], w_o[l], g_ffn[l], w_up[l], conv_w[l], conv_b[l], w_down[l])
    return rmsnorm(x, g_final)


def setup_inputs(seed: int = 0) -> dict:
    key = jax.random.key(seed)
    ks = jax.random.split(key, 16)
    f = jnp.float32
    L = DEPTH

    def nrm(k, shape, scale):
        return jax.random.normal(k, shape, f) * scale

    return {
        'x_prompt': nrm(ks[0], (BATCH, SEQ, D_MODEL), 1.0),
        'x_sample': nrm(ks[1], (DEC_BATCH, DEC_SEQ, D_MODEL), 1.0),
        'g_attn': 1.0 + nrm(ks[2], (L, D_MODEL), 0.02),
        'w_in': nrm(ks[3], (L, D_MODEL, IN_WIDTH), D_MODEL ** -0.5),
        'b_gate': nrm(ks[4], (L, N_BRANCH * D_MODEL), 0.02),
        'rel_bias': nrm(ks[5], (N_BUCKETS, N_HEADS_TOTAL), 0.5),
        'sink': nrm(ks[6], (L, B_Q_HEADS), 0.5),
        'w_a_out': nrm(ks[7], (L, A_OUT_WIDTH, D_MODEL), A_OUT_WIDTH ** -0.5),
        'w_b_out': nrm(ks[8], (L, B_Q_WIDTH, D_MODEL), B_Q_WIDTH ** -0.5),
        'w_o': nrm(ks[9], (L, D_MODEL, D_MODEL), D_MODEL ** -0.5),
        'g_ffn': 1.0 + nrm(ks[10], (L, D_MODEL), 0.02),
        'w_up': nrm(ks[11], (L, D_MODEL, 2 * D_FF), D_MODEL ** -0.5),
        'conv_w': nrm(ks[12], (L, CONV_WIDTH, 2 * D_FF), CONV_WIDTH ** -0.5),
        'conv_b': nrm(ks[13], (L, 2 * D_FF), 0.02),
        'w_down': nrm(ks[14], (L, D_FF, D_MODEL), D_FF ** -0.5),
        'g_final': 1.0 + nrm(ks[15], (D_MODEL,), 0.02),
    }


def reference(x_prompt, x_sample, g_attn, w_in, b_gate, rel_bias, sink, w_a_out, w_b_out, w_o,
              g_ffn, w_up, conv_w, conv_b, w_down, g_final):
    y_prompt = trunk(x_prompt, g_attn, w_in, b_gate, rel_bias, sink, w_a_out, w_b_out, w_o,
                     g_ffn, w_up, conv_w, conv_b, w_down, g_final)
    y_sample = trunk(x_sample, g_attn, w_in, b_gate, rel_bias, sink, w_a_out, w_b_out, w_o,
                     g_ffn, w_up, conv_w, conv_b, w_down, g_final)
    return (y_prompt, y_sample)
```

```python
import functools
import math

import jax
import jax.numpy as jnp
from jax import lax
from jax.experimental import pallas as pl
from jax.experimental.pallas import tpu as pltpu

F32 = jnp.float32
BF16 = jnp.bfloat16

D_MODEL = 1024
HEAD_DIM = 64
DIL_PATTERNS = ((128, 1), (512, 4), (2048, 16))
A_HEADS_PER_GROUP = 4
A_GROUP_WIDTH = A_HEADS_PER_GROUP * HEAD_DIM
N_GROUPS = len(DIL_PATTERNS)
A_HEADS = N_GROUPS * A_HEADS_PER_GROUP
A_WIDTH = A_HEADS * HEAD_DIM
A_SIDE = 64
B_Q_HEADS = 8
B_KV_HEADS = 2
B_GROUP = B_Q_HEADS // B_KV_HEADS
B_Q_WIDTH = B_Q_HEADS * HEAD_DIM
B_KV_WIDTH = B_KV_HEADS * HEAD_DIM
B_RADIUS = 128
N_BUCKETS = 32
MAX_DISTANCE = 1024
D_FF = 2816
EPS = 1e-6
NEG = -1e30
SCALE = HEAD_DIM ** -0.5

QBLK = 128
A_WIN = QBLK + 2 * A_SIDE
B_WIN = QBLK + 2 * B_RADIUS
GRP_COLS = 3 * A_GROUP_WIDTH
B_COLS = B_Q_WIDTH + 4 * 2 * HEAD_DIM
ROW_TILE = 512
FF_CHUNK = 256
HALO = 8
VMEM_LIMIT = 56 * 1024 * 1024


def _params(n_axes):
    return pltpu.CompilerParams(dimension_semantics=("arbitrary",) * n_axes,
                                vmem_limit_bytes=VMEM_LIMIT)


def _const_spec(shape):
    nd = len(shape)
    return pl.BlockSpec(shape, lambda *_: (0,) * nd, pipeline_mode=pl.Buffered(1))


def _rms(x, g):
    ms = jnp.mean(x * x, axis=-1, keepdims=True)
    return x * lax.rsqrt(ms + EPS) * g


def _qkv_kernel(x_ref, g_ref, w_ref, o1_ref, o2_ref, o3_ref, ob_ref):
    h = _rms(x_ref[...], g_ref[...]).astype(BF16)

    def proj(c0, n):
        return jnp.dot(h, w_ref[:, c0:c0 + n], preferred_element_type=F32)

    for gi, o_ref in enumerate((o1_ref, o2_ref, o3_ref)):
        base = gi * GRP_COLS
        o_ref[:, :A_GROUP_WIDTH] = (proj(base, A_GROUP_WIDTH) * SCALE).astype(BF16)
        o_ref[:, A_GROUP_WIDTH:] = proj(base + A_GROUP_WIDTH, 2 * A_GROUP_WIDTH).astype(BF16)
    base = N_GROUPS * GRP_COLS
    ob_ref[:, :B_Q_WIDTH] = (proj(base, B_Q_WIDTH) * SCALE).astype(BF16)
    ob_ref[:, B_Q_WIDTH:] = proj(base + B_Q_WIDTH, B_COLS - B_Q_WIDTH).astype(BF16)


def _qkv_call(x, g, w):
    n = x.shape[0]
    t = ROW_TILE
    row = lambda i: (i, 0)
    return pl.pallas_call(
        _qkv_kernel,
        grid=(n // t,),
        in_specs=[pl.BlockSpec((t, D_MODEL), row), _const_spec(g.shape), _const_spec(w.shape)],
        out_specs=[pl.BlockSpec((t, GRP_COLS), row)] * N_GROUPS + [pl.BlockSpec((t, B_COLS), row)],
        out_shape=[jax.ShapeDtypeStruct((n, GRP_COLS), BF16)] * N_GROUPS
        + [jax.ShapeDtypeStruct((n, B_COLS), BF16)],
        compiler_params=_params(1),
        name="qkv_proj",
    )(x, g, w)


def _window(i, nblk, m, win, side):
    if nblk == 1:
        return 0, 0
    start = pl.multiple_of(jnp.clip(i * QBLK - side, 0, m - win), 64)
    var = jnp.where(i == 0, 0, jnp.where(i == nblk - 1, 2, 1))
    return start, var


def _attn_a_kernel(qkv_ref, bias_ref, o_ref, lse_ref, *, m, r_per_step, win):
    nblk = m // QBLK
    lo = lax.broadcasted_iota(jnp.int32, (QBLK, 2 * HEAD_DIM), 1) < HEAD_DIM
    zero = jnp.zeros((QBLK, 2 * HEAD_DIM), BF16)

    def block(r, i):
        c = r * GRP_COLS
        q0 = pl.multiple_of(i * QBLK, QBLK)
        start, var = _window(i, nblk, m, win, A_SIDE)
        for hp in range(2):
            cq = c + hp * 2 * HEAD_DIM
            qp = qkv_ref[pl.ds(q0, QBLK), cq:cq + 2 * HEAD_DIM]
            kp = qkv_ref[pl.ds(start, win), cq + A_GROUP_WIDTH:cq + A_GROUP_WIDTH + 2 * HEAD_DIM]
            vp = qkv_ref[pl.ds(start, win), cq + 2 * A_GROUP_WIDTH:cq + 2 * A_GROUP_WIDTH + 2 * HEAD_DIM]
            res = []
            for hh in range(2):
                qm = jnp.where(lo if hh == 0 else jnp.logical_not(lo), qp, zero)
                s = lax.dot_general(qm, kp, (((1,), (1,)), ((), ())), preferred_element_type=F32)
                s = s + bias_ref[var, hp * 2 + hh]
                mx = jnp.max(s, axis=-1, keepdims=True)
                e = jnp.exp(s - mx)
                l = jnp.sum(e, axis=-1, keepdims=True)
                pv = jnp.dot(e.astype(BF16), vp, preferred_element_type=F32)
                res.append((pv / l, mx + jnp.log(l)))
            co = r * A_GROUP_WIDTH + hp * 2 * HEAD_DIM
            o_ref[pl.ds(q0, QBLK), co:co + 2 * HEAD_DIM] = jnp.where(lo, res[0][0], res[1][0]).astype(BF16)
            lse_ref[pl.ds(q0, QBLK), co:co + 2 * HEAD_DIM] = jnp.where(lo, res[0][1], res[1][1])

    for r in range(r_per_step):
        if nblk == 1:
            block(r, 0)
        else:
            lax.fori_loop(0, nblk, lambda i, carry, r=r: (block(r, i), carry)[1], 0)


def _attn_a_call(qkv_g, bias, b, s, d):
    m = s // d
    win = min(A_WIN, m)
    r_per_step = min(d, max(1, 1024 // m), 4)
    qkv_v = qkv_g.reshape(b, m, d * GRP_COLS)
    blk = lambda bi, ri: (bi, 0, ri)
    o, lse = pl.pallas_call(
        functools.partial(_attn_a_kernel, m=m, r_per_step=r_per_step, win=win),
        grid=(b, d // r_per_step),
        in_specs=[pl.BlockSpec((None, m, r_per_step * GRP_COLS), blk), _const_spec(bias.shape)],
        out_specs=[pl.BlockSpec((None, m, r_per_step * A_GROUP_WIDTH), blk)] * 2,
        out_shape=[jax.ShapeDtypeStruct((b, m, d * A_GROUP_WIDTH), BF16),
                   jax.ShapeDtypeStruct((b, m, d * A_GROUP_WIDTH), F32)],
        compiler_params=_params(2),
        name=f"attn_dil{d}",
    )(qkv_v, bias)
    return o.reshape(b * s, A_GROUP_WIDTH), lse.reshape(b * s, A_GROUP_WIDTH)


def _attn_b_kernel(q_ref, k_ref, v_ref, bias_ref, sink_ref, o_ref, *, s):
    nblk = s // QBLK
    lo = lax.broadcasted_iota(jnp.int32, (QBLK, 2 * HEAD_DIM), 1) < HEAD_DIM
    hi = jnp.logical_not(lo)
    zero = jnp.zeros((QBLK, 2 * HEAD_DIM), BF16)
    sink = sink_ref[...]

    def block(i, carry):
        q0 = pl.multiple_of(i * QBLK, QBLK)
        start, var = _window(i, nblk, s, B_WIN, B_RADIUS)
        qa = q_ref[pl.ds(q0, QBLK), :2 * HEAD_DIM]
        qb = q_ref[pl.ds(q0, QBLK), 2 * HEAD_DIM:]
        qs = jnp.concatenate([jnp.where(lo, qa, zero), jnp.where(hi, qa, zero),
                              jnp.where(lo, qb, zero), jnp.where(hi, qb, zero)], axis=0)
        kw = k_ref[pl.ds(start, B_WIN), :]
        vw = v_ref[pl.ds(start, B_WIN), :]
        sc = lax.dot_general(qs, kw, (((1,), (1,)), ((), ())), preferred_element_type=F32)
        sc = sc + bias_ref[var]
        mx = jnp.maximum(jnp.max(sc, axis=-1, keepdims=True), sink)
        e = jnp.exp(sc - mx)
        den = jnp.sum(e, axis=-1, keepdims=True) + jnp.exp(sink - mx)
        pv = jnp.dot(e.astype(BF16), vw, preferred_element_type=F32) / den
        o_ref[pl.ds(q0, QBLK), :2 * HEAD_DIM] = jnp.where(lo, pv[:QBLK], pv[QBLK:2 * QBLK]).astype(BF16)
        o_ref[pl.ds(q0, QBLK), 2 * HEAD_DIM:] = jnp.where(lo, pv[2 * QBLK:3 * QBLK], pv[3 * QBLK:]).astype(BF16)
        return carry

    lax.fori_loop(0, nblk, block, 0)


def _attn_b_call(qkv_b, bias, sink_rows, b, s):
    qkv_v = qkv_b.reshape(b, s, B_COLS)
    qw = B_GROUP * HEAD_DIM
    kv0 = B_Q_WIDTH // (2 * HEAD_DIM)
    o = pl.pallas_call(
        functools.partial(_attn_b_kernel, s=s),
        grid=(b, B_KV_HEADS),
        in_specs=[pl.BlockSpec((None, s, qw), lambda bi, c: (bi, 0, c)),
                  pl.BlockSpec((None, s, 2 * HEAD_DIM), lambda bi, c: (bi, 0, kv0 + c)),
                  pl.BlockSpec((None, s, 2 * HEAD_DIM), lambda bi, c: (bi, 0, kv0 + B_KV_HEADS + c)),
                  pl.BlockSpec((None,) + bias.shape[1:], lambda bi, c: (c, 0, 0, 0)),
                  pl.BlockSpec((None,) + sink_rows.shape[1:], lambda bi, c: (c, 0, 0))],
        out_specs=pl.BlockSpec((None, s, qw), lambda bi, c: (bi, 0, c)),
        out_shape=jax.ShapeDtypeStruct((b, s, B_Q_WIDTH), BF16),
        compiler_params=_params(2),
        name="attn_win",
    )(qkv_v, qkv_v, qkv_v, bias, sink_rows)
    return o.reshape(b * s, B_Q_WIDTH)


def _post_kernel(x_ref, o1_ref, o2_ref, o3_ref, l1_ref, l2_ref, l3_ref, bo_ref, g_ref, wg_ref,
                 bg_ref, wa_ref, wb_ref, wo_ref, out_ref, mix_ref):
    x = x_ref[...]
    h = _rms(x, g_ref[...]).astype(BF16)
    l1, l2, l3 = l1_ref[...], l2_ref[...], l3_ref[...]
    mx = jnp.maximum(jnp.maximum(l1, l2), l3)
    w1, w2, w3 = jnp.exp(l1 - mx), jnp.exp(l2 - mx), jnp.exp(l3 - mx)
    num = w1 * o1_ref[...].astype(F32) + w2 * o2_ref[...].astype(F32) + w3 * o3_ref[...].astype(F32)
    a = (num / (w1 + w2 + w3)).astype(BF16)
    bo = bo_ref[...]
    cw = 256
    for c0 in range(0, D_MODEL, cw):
        ap = jnp.dot(a, wa_ref[:, c0:c0 + cw], preferred_element_type=F32)
        bp = jnp.dot(bo, wb_ref[:, c0:c0 + cw], preferred_element_type=F32)
        ga = jnp.dot(h, wg_ref[:, c0:c0 + cw], preferred_element_type=F32) + bg_ref[:, c0:c0 + cw]
        gb = (jnp.dot(h, wg_ref[:, D_MODEL + c0:D_MODEL + c0 + cw], preferred_element_type=F32)
              + bg_ref[:, D_MODEL + c0:D_MODEL + c0 + cw])
        mix_ref[:, c0:c0 + cw] = (jax.nn.sigmoid(ga) * ap + jax.nn.sigmoid(gb) * bp).astype(BF16)
    out_ref[...] = x + jnp.dot(mix_ref[...], wo_ref[...], preferred_element_type=F32)


def _post_call(x, os_, ls_, bo, g, wg, bg, wa, wb, wo):
    n = x.shape[0]
    t = ROW_TILE
    row = lambda i: (i, 0)
    rows = lambda w: pl.BlockSpec((t, w), row)
    consts = (g, wg, bg, wa, wb, wo)
    return pl.pallas_call(
        _post_kernel,
        grid=(n // t,),
        in_specs=[rows(D_MODEL)] + [rows(A_GROUP_WIDTH)] * 6 + [rows(B_Q_WIDTH)]
        + [_const_spec(c.shape) for c in consts],
        out_specs=rows(D_MODEL),
        out_shape=jax.ShapeDtypeStruct((n, D_MODEL), F32),
        scratch_shapes=[pltpu.VMEM((t, D_MODEL), BF16)],
        compiler_params=_params(1),
        name="post_attn",
    )(x, *os_, *ls_, bo, *consts)


def _mlp_kernel(xp_ref, x_ref, xn_ref, g_ref, wup_ref, cw_ref, cb_ref, wdn_ref, gf_ref, out_ref,
                act_ref, *, tiles_per_seq):
    t = x_ref.shape[0]
    i = pl.program_id(0)
    first = (i % tiles_per_seq) == 0
    last = (i % tiles_per_seq) == tiles_per_seq - 1
    x = x_ref[...]
    xe = jnp.concatenate([xp_ref[...], x, xn_ref[...]], axis=0)
    he = _rms(xe, g_ref[...])
    rid = lax.broadcasted_iota(jnp.int32, (t + 2 * HALO, 1), 0)
    row_lo = jnp.where(first, HALO, 0)
    row_hi = jnp.where(last, t + HALO, t + 2 * HALO)
    he = jnp.where((rid >= row_lo) & (rid < row_hi), he, 0.0).astype(BF16)

    def conv(u, c0):
        w = cw_ref[:, c0:c0 + FF_CHUNK]
        return (u[HALO - 1:HALO - 1 + t] * w[0:1] + u[HALO:HALO + t] * w[1:2]
                + u[HALO + 1:HALO + 1 + t] * w[2:3] + cb_ref[:, c0:c0 + FF_CHUNK])

    for c0 in range(0, D_FF, FF_CHUNK):
        ug = jnp.dot(he, wup_ref[:, c0:c0 + FF_CHUNK], preferred_element_type=F32)
        uv = jnp.dot(he, wup_ref[:, D_FF + c0:D_FF + c0 + FF_CHUNK], preferred_element_type=F32)
        act_ref[:, c0:c0 + FF_CHUNK] = (jax.nn.gelu(conv(ug, c0)) * conv(uv, D_FF + c0)).astype(BF16)
    y = x + jnp.dot(act_ref[...], wdn_ref[...], preferred_element_type=F32)
    out_ref[...] = _rms(y, gf_ref[...])


def _mlp_call(x, s, g, wup, cw, cb, wdn, gf):
    n = x.shape[0]
    t = ROW_TILE
    hb = t // HALO
    nhb = n // HALO
    consts = (g, wup, cw, cb, wdn, gf)
    return pl.pallas_call(
        functools.partial(_mlp_kernel, tiles_per_seq=s // t),
        grid=(n // t,),
        in_specs=[pl.BlockSpec((HALO, D_MODEL), lambda i: (jnp.maximum(i * hb - 1, 0), 0)),
                  pl.BlockSpec((t, D_MODEL), lambda i: (i, 0)),
                  pl.BlockSpec((HALO, D_MODEL), lambda i: (jnp.minimum((i + 1) * hb, nhb - 1), 0))]
        + [_const_spec(c.shape) for c in consts],
        out_specs=pl.BlockSpec((t, D_MODEL), lambda i: (i, 0)),
        out_shape=jax.ShapeDtypeStruct((n, D_MODEL), F32),
        scratch_shapes=[pltpu.VMEM((t, D_FF), BF16)],
        compiler_params=_params(1),
        name="conv_mlp",
    )(x, x, x, *consts)


def _rel_bucket(rel):
    nb = N_BUCKETS // 2
    max_exact = nb // 2
    ret = jnp.where(rel > 0, nb, 0)
    n = jnp.abs(rel)
    nf = jnp.maximum(n, 1).astype(F32)
    large = max_exact + (jnp.log(nf / max_exact) / math.log(MAX_DISTANCE / max_exact)
                         * (nb - max_exact)).astype(jnp.int32)
    large = jnp.minimum(large, nb - 1)
    return ret + jnp.where(n < max_exact, n, large)


def _bias_tiles(rel_bias, heads, dilation, win, side, shifts):
    t = jnp.arange(QBLK, dtype=jnp.int32)[:, None]
    u = jnp.arange(win, dtype=jnp.int32)[None, :]
    tiles = []
    for sh in shifts:
        rel = u - sh - t
        vals = rel_bias[_rel_bucket(rel * dilation)][..., heads[0]:heads[-1] + 1]
        vals = jnp.where((jnp.abs(rel) <= side)[..., None], vals.astype(F32), NEG)
        tiles.append(vals.transpose(2, 0, 1))
    return jnp.stack(tiles, 0)


def _prep_layer(w_in, b_gate, rel_bias, sink, w_a_out, w_b_out, w_o, w_up, conv_w, conv_b, w_down,
                g_attn, g_ffn):
    cuts = [0, A_WIDTH, 2 * A_WIDTH, 3 * A_WIDTH, 3 * A_WIDTH + B_Q_WIDTH,
            3 * A_WIDTH + B_Q_WIDTH + B_KV_WIDTH, 3 * A_WIDTH + B_Q_WIDTH + 2 * B_KV_WIDTH]
    qa, ka, va, qb, kb, vb = (w_in[:, cuts[j]:cuts[j + 1]] for j in range(6))
    cols = []
    for gi in range(N_GROUPS):
        sl = slice(gi * A_GROUP_WIDTH, (gi + 1) * A_GROUP_WIDTH)
        cols += [qa[:, sl], ka[:, sl], va[:, sl]]
    cols.append(qb)
    for kv in (kb, vb):
        for c in range(B_KV_HEADS):
            one = kv[:, c * HEAD_DIM:(c + 1) * HEAD_DIM]
            cols += [one, one]
    w_qkv = jnp.concatenate(cols, axis=1).astype(BF16)
    p = dict(
        w_qkv=w_qkv,
        w_gate=w_in[:, cuts[6]:].astype(BF16),
        b_gate=b_gate.reshape(1, -1).astype(F32),
        w_a_out=w_a_out.astype(BF16), w_b_out=w_b_out.astype(BF16), w_o=w_o.astype(BF16),
        w_up=w_up.astype(BF16), conv_w=conv_w.astype(F32), conv_b=conv_b.reshape(1, -1).astype(F32),
        w_down=w_down.astype(BF16),
        g_attn=g_attn.reshape(1, -1).astype(F32), g_ffn=g_ffn.reshape(1, -1).astype(F32),
    )
    bb = _bias_tiles(rel_bias, list(range(A_HEADS, A_HEADS + B_Q_HEADS)), 1, B_WIN, B_RADIUS,
                     (0, B_RADIUS, 2 * B_RADIUS))
    p["bias_b"] = bb.reshape(3, B_KV_HEADS, B_GROUP * QBLK, B_WIN).transpose(1, 0, 2, 3)
    p["sink_rows"] = jnp.repeat(sink.astype(F32).reshape(B_KV_HEADS, B_GROUP), QBLK, axis=1)[..., None]
    p["rel_bias"] = rel_bias
    return p


def _layer(x, b, s, p):
    q1, q2, q3, qb = _qkv_call(x, p["g_attn"], p["w_qkv"])
    os_, ls_ = [], []
    for gi, ((_, d), qg) in enumerate(zip(DIL_PATTERNS, (q1, q2, q3))):
        m = s // d
        win = min(A_WIN, m)
        shifts = (0,) if m == QBLK else (0, A_SIDE, 2 * A_SIDE)
        heads = list(range(gi * A_HEADS_PER_GROUP, (gi + 1) * A_HEADS_PER_GROUP))
        bias = _bias_tiles(p["rel_bias"], heads, d, win, A_SIDE, shifts)
        o, lse = _attn_a_call(qg, bias, b, s, d)
        os_.append(o)
        ls_.append(lse)
    bo = _attn_b_call(qb, p["bias_b"], p["sink_rows"], b, s)
    x1 = _post_call(x, os_, ls_, bo, p["g_attn"], p["w_gate"], p["b_gate"], p["w_a_out"],
                    p["w_b_out"], p["w_o"])
    return x1


def _trunk(x, params, g_final):
    b, s, _ = x.shape
    (p,) = params
    x1 = _layer(x.reshape(b * s, D_MODEL), b, s, p)
    y = _mlp_call(x1, s, p["g_ffn"], p["w_up"], p["conv_w"], p["conv_b"], p["w_down"],
                  g_final.reshape(1, -1).astype(F32))
    return y.reshape(b, s, D_MODEL)


def kernel(x_prompt, x_sample, g_attn, w_in, b_gate, rel_bias, sink, w_a_out, w_b_out, w_o, g_ffn,
           w_up, conv_w, conv_b, w_down, g_final):
    depth = w_in.shape[0]
    assert depth == 1, "one layer supported"
    params = [_prep_layer(w_in[l], b_gate[l], rel_bias, sink[l], w_a_out[l], w_b_out[l], w_o[l],
                          w_up[l], conv_w[l], conv_b[l], w_down[l], g_attn[l], g_ffn[l])
              for l in range(depth)]
    return (_trunk(x_prompt, params, g_final), _trunk(x_sample, params, g_final))
```

```python
import functools
import math

import numpy as np
import jax
import jax.numpy as jnp
from jax import lax
from jax.experimental import pallas as pl
from jax.experimental.pallas import tpu as pltpu

F32 = jnp.float32
BF16 = jnp.bfloat16

D_MODEL = 1024
HEAD_DIM = 64
DIL_PATTERNS = ((128, 1), (512, 4), (2048, 16))
DILATIONS = tuple(d for _, d in DIL_PATTERNS)
A_HEADS_PER_GROUP = 4
A_GROUP_WIDTH = A_HEADS_PER_GROUP * HEAD_DIM
N_GROUPS = len(DIL_PATTERNS)
A_HEADS = N_GROUPS * A_HEADS_PER_GROUP
A_WIDTH = A_HEADS * HEAD_DIM
A_SIDE = 64
B_Q_HEADS = 8
B_KV_HEADS = 2
B_GROUP = B_Q_HEADS // B_KV_HEADS
B_Q_WIDTH = B_Q_HEADS * HEAD_DIM
B_KV_WIDTH = B_KV_HEADS * HEAD_DIM
B_RADIUS = 128
N_BUCKETS = 32
MAX_DISTANCE = 1024
D_FF = 2816
EPS = 1e-6
NEG = -1e30
SCALE = HEAD_DIM ** -0.5

KA_OFF, VA_OFF = A_WIDTH, 2 * A_WIDTH
QB_OFF = 3 * A_WIDTH
KVB_OFF = QB_OFF + B_Q_WIDTH
GATE_OFF = KVB_OFF + 2 * B_KV_WIDTH
QKV_COLS = GATE_OFF

PAIR = 2 * HEAD_DIM
PAIR_COLS = 3 * PAIR
GRP_COLS = 2 * PAIR_COLS
B_COLS = B_Q_WIDTH + 4 * PAIR
QBLK = 128
A_WIN = QBLK + 2 * A_SIDE
B_WIN = QBLK + 2 * B_RADIUS
ROW_TILE = 512
FF_CHUNK = 256
HALO = 8
VMEM_LIMIT = 56 * 1024 * 1024


def _params(n_axes):
    return pltpu.CompilerParams(dimension_semantics=("arbitrary",) * n_axes,
                                vmem_limit_bytes=VMEM_LIMIT)


def _const_spec(shape, index=None):
    index = (0,) * len(shape) if index is None else index
    return pl.BlockSpec(shape, lambda *_: index, pipeline_mode=pl.Buffered(1))


def _rms(x, g):
    ms = jnp.mean(x * x, axis=-1, keepdims=True)
    return x * lax.rsqrt(ms + EPS) * g


def _lane_lo():
    return lax.broadcasted_iota(jnp.int32, (QBLK, PAIR), 1) < HEAD_DIM


def _qkv_kernel(x_ref, g_ref, w_ref, o1_ref, o2_ref, o3_ref, ob_ref, tmp_ref):
    t = x_ref.shape[0]
    h = _rms(x_ref[...], g_ref[...]).astype(BF16)

    def proj(c0, n):
        return jnp.dot(h, w_ref[:, c0:c0 + n], preferred_element_type=F32)

    for gi, (o_ref, d) in enumerate(zip((o1_ref, o2_ref, o3_ref), DILATIONS)):
        c0 = gi * A_GROUP_WIDTH
        parts = (proj(c0, A_GROUP_WIDTH) * SCALE, proj(KA_OFF + c0, A_GROUP_WIDTH),
                 proj(VA_OFF + c0, A_GROUP_WIDTH))
        for p in range(2):
            for j, part in enumerate(parts):
                c = p * 3 + j
                val = part[:, p * PAIR:(p + 1) * PAIR]
                if d == 1:
                    o_ref[0, :, c * PAIR:(c + 1) * PAIR] = val.astype(BF16)
                else:
                    tmp_ref[c] = val
        if d > 1:
            for r in range(d):
                for c in range(GRP_COLS // PAIR):
                    o_ref[r, :, c * PAIR:(c + 1) * PAIR] = (
                        tmp_ref[c, pl.ds(r, t // d, stride=d), :].astype(BF16))

    ob_ref[:, :B_Q_WIDTH] = (proj(QB_OFF, B_Q_WIDTH) * SCALE).astype(BF16)
    kv = proj(KVB_OFF, 2 * B_KV_WIDTH)
    lo = lax.broadcasted_iota(jnp.int32, (t, PAIR), 1) < HEAD_DIM
    for j in range(2):
        one = kv[:, j * PAIR:(j + 1) * PAIR]
        swapped = pltpu.roll(one, HEAD_DIM, axis=1)
        c0 = B_Q_WIDTH + j * 2 * PAIR
        ob_ref[:, c0:c0 + PAIR] = jnp.where(lo, one, swapped).astype(BF16)
        ob_ref[:, c0 + PAIR:c0 + 2 * PAIR] = jnp.where(lo, swapped, one).astype(BF16)


def _qkv_call(x, g, w_in, b, s):
    n = x.shape[0]
    t = ROW_TILE
    tps = s // t
    grp_shape = lambda d: jax.ShapeDtypeStruct((b, d, s // d, GRP_COLS), BF16)
    grp_spec = lambda d: pl.BlockSpec((None, d, t // d, GRP_COLS), lambda i: (i // tps, 0, i % tps, 0))
    return pl.pallas_call(
        _qkv_kernel,
        grid=(n // t,),
        in_specs=[pl.BlockSpec((t, D_MODEL), lambda i: (i, 0)), _const_spec(g.shape),
                  _const_spec((D_MODEL, QKV_COLS))],
        out_specs=[grp_spec(d) for d in DILATIONS] + [pl.BlockSpec((t, B_COLS), lambda i: (i, 0))],
        out_shape=[grp_shape(d) for d in DILATIONS] + [jax.ShapeDtypeStruct((n, B_COLS), BF16)],
        scratch_shapes=[pltpu.VMEM((GRP_COLS // PAIR, t, PAIR), F32)],
        compiler_params=_params(1),
        name="qkv_proj",
    )(x, g, w_in)


def _window(i, nblk, m, win, side):
    if nblk == 1:
        return 0, 0
    start = pl.multiple_of(jnp.clip(i * QBLK - side, 0, m - win), 64)
    var = jnp.where(i == 0, 0, jnp.where(i == nblk - 1, 2, 1))
    return start, var


def _attn_a_kernel(g1_ref, g2_ref, g3_ref, b1_ref, b2_ref, b3_ref, out_ref, acc_ref, m_ref, l_ref, *, s):
    lo = _lane_lo()
    hi = jnp.logical_not(lo)
    zero = jnp.zeros((QBLK, PAIR), BF16)

    def attend(q, k, v, bias):
        qs = jnp.concatenate([jnp.where(lo, q, zero), jnp.where(hi, q, zero)], axis=0)
        sc = lax.dot_general(qs, k, (((1,), (1,)), ((), ())), preferred_element_type=F32) + bias
        mx = jnp.max(sc, axis=-1, keepdims=True)
        e = jnp.exp(sc - mx)
        l = jnp.sum(e, axis=-1, keepdims=True)
        pv = jnp.dot(e.astype(BF16), v, preferred_element_type=F32)
        return (jnp.where(lo, pv[:QBLK], pv[QBLK:]), jnp.where(lo, mx[:QBLK], mx[QBLK:]),
                jnp.where(lo, l[:QBLK], l[QBLK:]))

    for gi, (ref, bias_ref, d) in enumerate(zip((g1_ref, g2_ref, g3_ref), (b1_ref, b2_ref, b3_ref),
                                                DILATIONS)):
        m = s // d
        nblk = m // QBLK
        win = min(A_WIN, m)
        shift = nblk.bit_length() - 1
        assert nblk == 1 << shift

        def body(j, carry, ref=ref, bias_ref=bias_ref, d=d, m=m, nblk=nblk, win=win, shift=shift, gi=gi):
            r, i = (0, j) if d == 1 else (j >> shift, j & (nblk - 1))
            q0 = pl.multiple_of(i * QBLK, QBLK)
            start, var = _window(i, nblk, m, win, A_SIDE)
            o, mm, ll = attend(ref[r, pl.ds(q0, QBLK), :PAIR], ref[r, pl.ds(start, win), PAIR:2 * PAIR],
                               ref[r, pl.ds(start, win), 2 * PAIR:], bias_ref[var])
            rows = pl.ds(q0, QBLK) if d == 1 else pl.ds(r + d * q0, QBLK, stride=d)
            if gi == 0:
                acc_ref[rows, :] = o
                m_ref[rows, :] = mm
                l_ref[rows, :] = ll
            else:
                m_old = m_ref[rows, :]
                m_new = jnp.maximum(m_old, mm)
                alpha = jnp.exp(m_old - m_new)
                beta = jnp.exp(mm - m_new)
                m_ref[rows, :] = m_new
                l_ref[rows, :] = alpha * l_ref[rows, :] + beta * ll
                acc_ref[rows, :] = alpha * acc_ref[rows, :] + beta * o
            return carry

        lax.fori_loop(0, d * nblk, body, 0)

    out_ref[...] = (acc_ref[...] / l_ref[...]).astype(BF16)


def _attn_a_call(groups, biases, b, s):
    in_specs = [pl.BlockSpec((None, d, s // d, PAIR_COLS), lambda bi, p: (bi, 0, 0, p)) for d in DILATIONS]
    in_specs += [pl.BlockSpec((bias.shape[0], None) + bias.shape[2:], lambda bi, p: (0, p, 0, 0))
                 for bias in biases]
    out = pl.pallas_call(
        functools.partial(_attn_a_kernel, s=s),
        grid=(b, 2),
        in_specs=in_specs,
        out_specs=pl.BlockSpec((None, s, PAIR), lambda bi, p: (bi, 0, p)),
        out_shape=jax.ShapeDtypeStruct((b, s, A_GROUP_WIDTH), BF16),
        scratch_shapes=[pltpu.VMEM((s, PAIR), F32)] * 3,
        compiler_params=_params(2),
        name="attn_dil",
    )(*groups, *biases)
    return out.reshape(b * s, A_GROUP_WIDTH)


def _attn_b_kernel(q_ref, k_ref, v_ref, bias_ref, sink_ref, o_ref, *, s):
    nblk = s // QBLK
    lo = _lane_lo()
    hi = jnp.logical_not(lo)
    zero = jnp.zeros((QBLK, PAIR), BF16)
    sink = sink_ref[...]

    def block(i, carry):
        q0 = pl.multiple_of(i * QBLK, QBLK)
        start, var = _window(i, nblk, s, B_WIN, B_RADIUS)
        qa = q_ref[pl.ds(q0, QBLK), :PAIR]
        qb = q_ref[pl.ds(q0, QBLK), PAIR:]
        qs = jnp.concatenate([jnp.where(lo, qa, zero), jnp.where(hi, qa, zero),
                              jnp.where(lo, qb, zero), jnp.where(hi, qb, zero)], axis=0)
        kw = k_ref[pl.ds(start, B_WIN), :]
        vw = v_ref[pl.ds(start, B_WIN), :]
        sc = lax.dot_general(qs, kw, (((1,), (1,)), ((), ())), preferred_element_type=F32)
        sc = sc + bias_ref[var]
        mx = jnp.maximum(jnp.max(sc, axis=-1, keepdims=True), sink)
        e = jnp.exp(sc - mx)
        den = jnp.sum(e, axis=-1, keepdims=True) + jnp.exp(sink - mx)
        pv = jnp.dot(e.astype(BF16), vw, preferred_element_type=F32) / den
        o_ref[pl.ds(q0, QBLK), :PAIR] = jnp.where(lo, pv[:QBLK], pv[QBLK:2 * QBLK]).astype(BF16)
        o_ref[pl.ds(q0, QBLK), PAIR:] = jnp.where(lo, pv[2 * QBLK:3 * QBLK], pv[3 * QBLK:]).astype(BF16)
        return carry

    lax.fori_loop(0, nblk, block, 0, unroll=2)


def _attn_b_call(qkv_b, bias, sink_rows, b, s):
    qkv_v = qkv_b.reshape(b, s, B_COLS)
    qw = B_GROUP * HEAD_DIM
    kv0 = B_Q_WIDTH // PAIR
    o = pl.pallas_call(
        functools.partial(_attn_b_kernel, s=s),
        grid=(b, B_KV_HEADS),
        in_specs=[pl.BlockSpec((None, s, qw), lambda bi, c: (bi, 0, c)),
                  pl.BlockSpec((None, s, PAIR), lambda bi, c: (bi, 0, kv0 + c)),
                  pl.BlockSpec((None, s, PAIR), lambda bi, c: (bi, 0, kv0 + B_KV_HEADS + c)),
                  pl.BlockSpec((None,) + bias.shape[1:], lambda bi, c: (c, 0, 0, 0)),
                  pl.BlockSpec((None,) + sink_rows.shape[1:], lambda bi, c: (c, 0, 0))],
        out_specs=pl.BlockSpec((None, s, qw), lambda bi, c: (bi, 0, c)),
        out_shape=jax.ShapeDtypeStruct((b, s, B_Q_WIDTH), BF16),
        compiler_params=_params(2),
        name="attn_win",
    )(qkv_v, qkv_v, qkv_v, bias, sink_rows)
    return o.reshape(b * s, B_Q_WIDTH)


def _post_kernel(x_ref, a_ref, bo_ref, g_ref, wga_ref, wgb_ref, bg_ref, wa_ref, wb_ref, wo_ref,
                 out_ref, mix_ref):
    x = x_ref[...]
    h = _rms(x, g_ref[...]).astype(BF16)
    a = a_ref[...]
    bo = bo_ref[...]
    cw = 256
    for c0 in range(0, D_MODEL, cw):
        cs = slice(c0, c0 + cw)
        ap = jnp.dot(a, wa_ref[:, cs], preferred_element_type=F32)
        bp = jnp.dot(bo, wb_ref[:, cs], preferred_element_type=F32)
        ga = jnp.dot(h, wga_ref[:, cs], preferred_element_type=F32) + bg_ref[:, cs]
        gb = jnp.dot(h, wgb_ref[:, cs], preferred_element_type=F32) + bg_ref[:, D_MODEL + c0:D_MODEL + c0 + cw]
        mix_ref[:, cs] = (jax.nn.sigmoid(ga) * ap + jax.nn.sigmoid(gb) * bp).astype(BF16)
    out_ref[...] = x + jnp.dot(mix_ref[...], wo_ref[...], preferred_element_type=F32)


def _post_call(x, a, bo, g, w_in, bg, wa, wb, wo):
    n = x.shape[0]
    t = ROW_TILE
    rows = lambda w: pl.BlockSpec((t, w), lambda i: (i, 0))
    gate_blk = GATE_OFF // D_MODEL
    return pl.pallas_call(
        _post_kernel,
        grid=(n // t,),
        in_specs=[rows(D_MODEL), rows(A_GROUP_WIDTH), rows(B_Q_WIDTH), _const_spec(g.shape),
                  _const_spec((D_MODEL, D_MODEL), (0, gate_blk)),
                  _const_spec((D_MODEL, D_MODEL), (0, gate_blk + 1)),
                  _const_spec(bg.shape), _const_spec(wa.shape), _const_spec(wb.shape), _const_spec(wo.shape)],
        out_specs=rows(D_MODEL),
        out_shape=jax.ShapeDtypeStruct((n, D_MODEL), F32),
        scratch_shapes=[pltpu.VMEM((t, D_MODEL), BF16)],
        compiler_params=_params(1),
        name="post_attn",
    )(x, a, bo, g, w_in, w_in, bg, wa, wb, wo)


def _mlp_kernel(xp_ref, x_ref, xn_ref, g_ref, wup_ref, cw_ref, cb_ref, wdn_ref, gf_ref, out_ref,
                act_ref, *, tiles_per_seq):
    t = x_ref.shape[0]
    i = pl.program_id(0)
    first = (i % tiles_per_seq) == 0
    last = (i % tiles_per_seq) == tiles_per_seq - 1
    x = x_ref[...]
    xe = jnp.concatenate([xp_ref[...], x, xn_ref[...]], axis=0)
    he = _rms(xe, g_ref[...])
    rid = lax.broadcasted_iota(jnp.int32, (t + 2 * HALO, 1), 0)
    row_lo = jnp.where(first, HALO, 0)
    row_hi = jnp.where(last, t + HALO, t + 2 * HALO)
    he = jnp.where((rid >= row_lo) & (rid < row_hi), he, 0.0).astype(BF16)

    def conv(u, c0):
        w = cw_ref[:, c0:c0 + FF_CHUNK]
        return (u[HALO - 1:HALO - 1 + t] * w[0:1] + u[HALO:HALO + t] * w[1:2]
                + u[HALO + 1:HALO + 1 + t] * w[2:3] + cb_ref[:, c0:c0 + FF_CHUNK])

    for c0 in range(0, D_FF, FF_CHUNK):
        ug = jnp.dot(he, wup_ref[:, c0:c0 + FF_CHUNK], preferred_element_type=F32)
        uv = jnp.dot(he, wup_ref[:, D_FF + c0:D_FF + c0 + FF_CHUNK], preferred_element_type=F32)
        act_ref[:, c0:c0 + FF_CHUNK] = (jax.nn.gelu(conv(ug, c0)) * conv(uv, D_FF + c0)).astype(BF16)
    y = x + jnp.dot(act_ref[...], wdn_ref[...], preferred_element_type=F32)
    out_ref[...] = _rms(y, gf_ref[...])


def _mlp_call(x, s, g, wup, cw, cb, wdn, gf):
    n = x.shape[0]
    t = ROW_TILE
    hb = t // HALO
    nhb = n // HALO
    consts = (g, wup, cw, cb, wdn, gf)
    return pl.pallas_call(
        functools.partial(_mlp_kernel, tiles_per_seq=s // t),
        grid=(n // t,),
        in_specs=[pl.BlockSpec((HALO, D_MODEL), lambda i: (jnp.maximum(i * hb - 1, 0), 0)),
                  pl.BlockSpec((t, D_MODEL), lambda i: (i, 0)),
                  pl.BlockSpec((HALO, D_MODEL), lambda i: (jnp.minimum((i + 1) * hb, nhb - 1), 0))]
        + [_const_spec(c.shape) for c in consts],
        out_specs=pl.BlockSpec((t, D_MODEL), lambda i: (i, 0)),
        out_shape=jax.ShapeDtypeStruct((n, D_MODEL), F32),
        scratch_shapes=[pltpu.VMEM((t, D_FF), BF16)],
        compiler_params=_params(1),
        name="conv_mlp",
    )(x, x, x, *consts)


def _rel_bucket(rel):
    nb = N_BUCKETS // 2
    max_exact = nb // 2
    rel = np.asarray(rel, np.int32)
    ret = np.where(rel > 0, nb, 0)
    n = np.abs(rel)
    nf = np.maximum(n, 1).astype(np.float32)
    large = max_exact + (np.log(nf / np.float32(max_exact)) / np.float32(math.log(MAX_DISTANCE / max_exact))
                         * np.float32(nb - max_exact)).astype(np.int32)
    large = np.minimum(large, nb - 1)
    return (ret + np.where(n < max_exact, n, large)).astype(np.int32)


def _bias_tiles(rel_bias, head0, n_heads, dilation, win, side, shifts):
    offs = np.arange(-side, side + 1) * dilation
    line = rel_bias[_rel_bucket(offs)][:, head0:head0 + n_heads].T.astype(F32)
    k0 = QBLK + max(shifts) - side
    p = win + k0 + side + 1
    padded = jnp.pad(line, ((0, 0), (k0, p - k0 - (2 * side + 1))), constant_values=NEG)
    skew = jnp.tile(padded, (1, QBLK + 1))[:, :QBLK * (p - 1)].reshape(n_heads, QBLK, p - 1)
    return jnp.stack([skew[:, :, k0 - sh + side:k0 - sh + side + win] for sh in shifts], 0)


def _prep(w_in, b_gate, rel_bias, sink, w_a_out, w_b_out, w_o, w_up, conv_w, conv_b, w_down,
          g_attn, g_ffn, g_final):
    row = lambda v: v.reshape(1, -1).astype(F32)
    p = dict(
        w_in=w_in.astype(BF16), b_gate=row(b_gate),
        w_a_out=w_a_out.astype(BF16), w_b_out=w_b_out.astype(BF16), w_o=w_o.astype(BF16),
        w_up=w_up.astype(BF16), conv_w=conv_w.astype(F32), conv_b=row(conv_b),
        w_down=w_down.astype(BF16), g_attn=row(g_attn), g_ffn=row(g_ffn), g_final=row(g_final),
        rel_bias=rel_bias,
    )
    bb = _bias_tiles(rel_bias, A_HEADS, B_Q_HEADS, 1, B_WIN, B_RADIUS, (0, B_RADIUS, 2 * B_RADIUS))
    p["bias_b"] = bb.reshape(3, B_KV_HEADS, B_GROUP * QBLK, B_WIN).transpose(1, 0, 2, 3)
    p["sink_rows"] = jnp.repeat(sink.astype(F32).reshape(B_KV_HEADS, B_GROUP), QBLK, axis=1)[..., None]
    return p


def _group_biases(rel_bias, s):
    out = []
    for gi, d in enumerate(DILATIONS):
        m = s // d
        win = min(A_WIN, m)
        shifts = (0,) if m == QBLK else (0, A_SIDE, 2 * A_SIDE)
        t = _bias_tiles(rel_bias, gi * A_HEADS_PER_GROUP, A_HEADS_PER_GROUP, d, win, A_SIDE, shifts)
        out.append(t.reshape(len(shifts), 2, 2 * QBLK, win))
    return out


def _trunk(x, p):
    b, s, _ = x.shape
    xf = x.reshape(b * s, D_MODEL)
    g1, g2, g3, qb = _qkv_call(xf, p["g_attn"], p["w_in"], b, s)
    a = _attn_a_call((g1, g2, g3), _group_biases(p["rel_bias"], s), b, s)
    bo = _attn_b_call(qb, p["bias_b"], p["sink_rows"], b, s)
    x1 = _post_call(xf, a, bo, p["g_attn"], p["w_in"], p["b_gate"], p["w_a_out"], p["w_b_out"], p["w_o"])
    y = _mlp_call(x1, s, p["g_ffn"], p["w_up"], p["conv_w"], p["conv_b"], p["w_down"], p["g_final"])
    return y.reshape(b, s, D_MODEL)


def kernel(x_prompt, x_sample, g_attn, w_in, b_gate, rel_bias, sink, w_a_out, w_b_out, w_o, g_ffn,
           w_up, conv_w, conv_b, w_down, g_final):
    assert w_in.shape[0] == 1, "one layer: the final RMSNorm is fused into its MLP kernel"
    p = _prep(w_in[0], b_gate[0], rel_bias, sink[0], w_a_out[0], w_b_out[0], w_o[0], w_up[0],
              conv_w[0], conv_b[0], w_down[0], g_attn[0], g_ffn[0], g_final)
    return (_trunk(x_prompt, p), _trunk(x_sample, p))
```

```python
import functools
import math

import numpy as np
import jax
import jax.numpy as jnp
from jax import lax
from jax.experimental import pallas as pl
from jax.experimental.pallas import tpu as pltpu

F32 = jnp.float32
BF16 = jnp.bfloat16

D_MODEL = 1024
HEAD_DIM = 64
DIL_PATTERNS = ((128, 1), (512, 4), (2048, 16))
DILATIONS = tuple(d for _, d in DIL_PATTERNS)
A_HEADS_PER_GROUP = 4
A_GROUP_WIDTH = A_HEADS_PER_GROUP * HEAD_DIM
N_GROUPS = len(DIL_PATTERNS)
A_HEADS = N_GROUPS * A_HEADS_PER_GROUP
A_WIDTH = A_HEADS * HEAD_DIM
A_SIDE = 64
B_Q_HEADS = 8
B_KV_HEADS = 2
B_GROUP = B_Q_HEADS // B_KV_HEADS
B_Q_WIDTH = B_Q_HEADS * HEAD_DIM
B_KV_WIDTH = B_KV_HEADS * HEAD_DIM
B_RADIUS = 128
N_BUCKETS = 32
MAX_DISTANCE = 1024
D_FF = 2816
EPS = 1e-6
NEG = -1e30
SCALE = HEAD_DIM ** -0.5

KA_OFF, VA_OFF = A_WIDTH, 2 * A_WIDTH
QB_OFF = 3 * A_WIDTH
KVB_OFF = QB_OFF + B_Q_WIDTH
GATE_OFF = KVB_OFF + 2 * B_KV_WIDTH
QKV_COLS = GATE_OFF

PAIR = 2 * HEAD_DIM
PAIR_COLS = 3 * PAIR
GRP_COLS = 2 * PAIR_COLS
B_COLS = B_Q_WIDTH + 4 * PAIR
QBLK = 128
A_WIN = QBLK + 2 * A_SIDE
B_WIN = QBLK + 2 * B_RADIUS
ROW_TILE = 512
FF_CHUNK = 256
HALO = 8
VMEM_LIMIT = 56 * 1024 * 1024


def _params(n_axes):
    return pltpu.CompilerParams(dimension_semantics=("arbitrary",) * n_axes,
                                vmem_limit_bytes=VMEM_LIMIT)


def _const_spec(shape, index=None):
    index = (0,) * len(shape) if index is None else index
    return pl.BlockSpec(shape, lambda *_: index, pipeline_mode=pl.Buffered(1))


def _rms(x, g):
    ms = jnp.mean(x * x, axis=-1, keepdims=True)
    return x * lax.rsqrt(ms + EPS) * g


def _lane_lo(rows=QBLK):
    return lax.broadcasted_iota(jnp.int32, (rows, PAIR), 1) < HEAD_DIM


def _pair_rhs(v):
    lo = _lane_lo(v.shape[0])
    zero = jnp.zeros_like(v)
    ones_lo = jnp.where(lo, 1.0, 0.0).astype(v.dtype)
    ones_hi = jnp.where(lo, 0.0, 1.0).astype(v.dtype)
    top = jnp.concatenate([jnp.where(lo, v, zero), ones_lo], axis=1)
    bot = jnp.concatenate([jnp.where(lo, zero, v), ones_hi], axis=1)
    return jnp.concatenate([top, bot], axis=0)


def _qkv_kernel(x_ref, g_ref, w_ref, o1_ref, o2_ref, o3_ref, ob_ref, tmp_ref):
    t = x_ref.shape[0]
    h = _rms(x_ref[...], g_ref[...]).astype(BF16)

    def proj(c0, n):
        return jnp.dot(h, w_ref[:, c0:c0 + n], preferred_element_type=F32)

    for gi, (o_ref, d) in enumerate(zip((o1_ref, o2_ref, o3_ref), DILATIONS)):
        c0 = gi * A_GROUP_WIDTH
        parts = (proj(c0, A_GROUP_WIDTH) * SCALE, proj(KA_OFF + c0, A_GROUP_WIDTH),
                 proj(VA_OFF + c0, A_GROUP_WIDTH))
        for p in range(2):
            for j, part in enumerate(parts):
                c = p * 3 + j
                val = part[:, p * PAIR:(p + 1) * PAIR]
                if d == 1:
                    o_ref[0, :, c * PAIR:(c + 1) * PAIR] = val.astype(BF16)
                else:
                    tmp_ref[c] = val
        if d > 1:
            for r in range(d):
                for c in range(GRP_COLS // PAIR):
                    o_ref[r, :, c * PAIR:(c + 1) * PAIR] = (
                        tmp_ref[c, pl.ds(r, t // d, stride=d), :].astype(BF16))

    ob_ref[:, :B_Q_WIDTH] = (proj(QB_OFF, B_Q_WIDTH) * SCALE).astype(BF16)
    kv = proj(KVB_OFF, 2 * B_KV_WIDTH)
    lo = lax.broadcasted_iota(jnp.int32, (t, PAIR), 1) < HEAD_DIM
    for j in range(2):
        one = kv[:, j * PAIR:(j + 1) * PAIR]
        swapped = pltpu.roll(one, HEAD_DIM, axis=1)
        c0 = B_Q_WIDTH + j * 2 * PAIR
        ob_ref[:, c0:c0 + PAIR] = jnp.where(lo, one, swapped).astype(BF16)
        ob_ref[:, c0 + PAIR:c0 + 2 * PAIR] = jnp.where(lo, swapped, one).astype(BF16)


def _qkv_call(x, g, w_in, b, s):
    n = x.shape[0]
    t = ROW_TILE
    tps = s // t
    grp_shape = lambda d: jax.ShapeDtypeStruct((b, d, s // d, GRP_COLS), BF16)
    grp_spec = lambda d: pl.BlockSpec((None, d, t // d, GRP_COLS), lambda i: (i // tps, 0, i % tps, 0))
    return pl.pallas_call(
        _qkv_kernel,
        grid=(n // t,),
        in_specs=[pl.BlockSpec((t, D_MODEL), lambda i: (i, 0)), _const_spec(g.shape),
                  _const_spec((D_MODEL, QKV_COLS))],
        out_specs=[grp_spec(d) for d in DILATIONS] + [pl.BlockSpec((t, B_COLS), lambda i: (i, 0))],
        out_shape=[grp_shape(d) for d in DILATIONS] + [jax.ShapeDtypeStruct((n, B_COLS), BF16)],
        scratch_shapes=[pltpu.VMEM((GRP_COLS // PAIR, t, PAIR), F32)],
        compiler_params=_params(1),
        name="qkv_proj",
    )(x, g, w_in)


def _window(i, nblk, m, win, side):
    if nblk == 1:
        return 0, 0
    start = pl.multiple_of(jnp.clip(i * QBLK - side, 0, m - win), 64)
    var = jnp.where(i == 0, 0, jnp.where(i == nblk - 1, 2, 1))
    return start, var


def _attn_a_kernel(g1_ref, g2_ref, g3_ref, b1_ref, b2_ref, b3_ref, out_ref, acc_ref, m_ref, l_ref, *, s):
    lo = _lane_lo()
    hi = jnp.logical_not(lo)
    zero = jnp.zeros((QBLK, PAIR), BF16)

    def attend(q, k, v, bias):
        qs = jnp.concatenate([jnp.where(lo, q, zero), jnp.where(hi, q, zero)], axis=0)
        sc = lax.dot_general(qs, k, (((1,), (1,)), ((), ())), preferred_element_type=F32) + bias
        mx = jnp.max(sc, axis=-1, keepdims=True)
        e = jnp.exp(sc - mx).astype(BF16)
        r = jnp.dot(jnp.concatenate([e[:QBLK], e[QBLK:]], axis=1), _pair_rhs(v), preferred_element_type=F32)
        return r[:, :PAIR], jnp.where(lo, mx[:QBLK], mx[QBLK:]), r[:, PAIR:]

    for gi, (ref, bias_ref, d) in enumerate(zip((g1_ref, g2_ref, g3_ref), (b1_ref, b2_ref, b3_ref),
                                                DILATIONS)):
        m = s // d
        nblk = m // QBLK
        win = min(A_WIN, m)
        shift = nblk.bit_length() - 1
        assert nblk == 1 << shift

        def body(j, carry, ref=ref, bias_ref=bias_ref, d=d, m=m, nblk=nblk, win=win, shift=shift, gi=gi):
            r, i = (0, j) if d == 1 else (j >> shift, j & (nblk - 1))
            q0 = pl.multiple_of(i * QBLK, QBLK)
            start, var = _window(i, nblk, m, win, A_SIDE)
            o, mm, ll = attend(ref[r, pl.ds(q0, QBLK), :PAIR], ref[r, pl.ds(start, win), PAIR:2 * PAIR],
                               ref[r, pl.ds(start, win), 2 * PAIR:], bias_ref[var])
            rows = pl.ds(q0, QBLK) if d == 1 else pl.ds(r + d * q0, QBLK, stride=d)
            if gi == 0:
                acc_ref[rows, :] = o
                m_ref[rows, :] = mm
                l_ref[rows, :] = ll
            else:
                m_old = m_ref[rows, :]
                m_new = jnp.maximum(m_old, mm)
                alpha = jnp.exp(m_old - m_new)
                beta = jnp.exp(mm - m_new)
                m_ref[rows, :] = m_new
                l_ref[rows, :] = alpha * l_ref[rows, :] + beta * ll
                acc_ref[rows, :] = alpha * acc_ref[rows, :] + beta * o
            return carry

        lax.fori_loop(0, d * nblk, body, 0, unroll=2)

    out_ref[...] = (acc_ref[...] / l_ref[...]).astype(BF16)


def _attn_a_call(groups, biases, b, s):
    in_specs = [pl.BlockSpec((None, d, s // d, PAIR_COLS), lambda bi, p: (bi, 0, 0, p)) for d in DILATIONS]
    in_specs += [pl.BlockSpec((bias.shape[0], None) + bias.shape[2:], lambda bi, p: (0, p, 0, 0))
                 for bias in biases]
    out = pl.pallas_call(
        functools.partial(_attn_a_kernel, s=s),
        grid=(b, 2),
        in_specs=in_specs,
        out_specs=pl.BlockSpec((None, s, PAIR), lambda bi, p: (bi, 0, p)),
        out_shape=jax.ShapeDtypeStruct((b, s, A_GROUP_WIDTH), BF16),
        scratch_shapes=[pltpu.VMEM((s, PAIR), F32)] * 3,
        compiler_params=_params(2),
        name="attn_dil",
    )(*groups, *biases)
    return out.reshape(b * s, A_GROUP_WIDTH)


def _attn_b_kernel(q_ref, k_ref, v_ref, bias_ref, sink_ref, o_ref, *, s):
    nblk = s // QBLK
    lo = _lane_lo()
    hi = jnp.logical_not(lo)
    zero = jnp.zeros((QBLK, PAIR), BF16)
    c = pl.program_id(1)

    def block(i, carry):
        q0 = pl.multiple_of(i * QBLK, QBLK)
        start, var = _window(i, nblk, s, B_WIN, B_RADIUS)
        kw = k_ref[pl.ds(start, B_WIN), :]
        rhs = _pair_rhs(v_ref[pl.ds(start, B_WIN), :])
        for p in range(2):
            q = q_ref[pl.ds(q0, QBLK), p * PAIR:(p + 1) * PAIR]
            qs = jnp.concatenate([jnp.where(lo, q, zero), jnp.where(hi, q, zero)], axis=0)
            sc = lax.dot_general(qs, kw, (((1,), (1,)), ((), ())), preferred_element_type=F32)
            sc = sc + bias_ref[var, p * 2 * QBLK:(p + 1) * 2 * QBLK]
            sk_e = sink_ref[c * B_GROUP + 2 * p]
            sk_o = sink_ref[c * B_GROUP + 2 * p + 1]
            mx = jnp.max(sc, axis=-1, keepdims=True)
            mx = jnp.concatenate([jnp.maximum(mx[:QBLK], sk_e), jnp.maximum(mx[QBLK:], sk_o)], axis=0)
            e = jnp.exp(sc - mx).astype(BF16)
            r = jnp.dot(jnp.concatenate([e[:QBLK], e[QBLK:]], axis=1), rhs, preferred_element_type=F32)
            den = r[:, PAIR:] + jnp.where(lo, jnp.exp(sk_e - mx[:QBLK]), jnp.exp(sk_o - mx[QBLK:]))
            o_ref[pl.ds(q0, QBLK), p * PAIR:(p + 1) * PAIR] = (r[:, :PAIR] / den).astype(BF16)
        return carry

    lax.fori_loop(0, nblk, block, 0, unroll=2)


def _attn_b_call(qkv_b, bias, sink, b, s):
    qkv_v = qkv_b.reshape(b, s, B_COLS)
    qw = B_GROUP * HEAD_DIM
    kv0 = B_Q_WIDTH // PAIR
    o = pl.pallas_call(
        functools.partial(_attn_b_kernel, s=s),
        grid=(b, B_KV_HEADS),
        in_specs=[pl.BlockSpec((None, s, qw), lambda bi, c: (bi, 0, c)),
                  pl.BlockSpec((None, s, PAIR), lambda bi, c: (bi, 0, kv0 + c)),
                  pl.BlockSpec((None, s, PAIR), lambda bi, c: (bi, 0, kv0 + B_KV_HEADS + c)),
                  pl.BlockSpec((None,) + bias.shape[1:], lambda bi, c: (c, 0, 0, 0)),
                  pl.BlockSpec(memory_space=pltpu.SMEM)],
        out_specs=pl.BlockSpec((None, s, qw), lambda bi, c: (bi, 0, c)),
        out_shape=jax.ShapeDtypeStruct((b, s, B_Q_WIDTH), BF16),
        compiler_params=_params(2),
        name="attn_win",
    )(qkv_v, qkv_v, qkv_v, bias, sink)
    return o.reshape(b * s, B_Q_WIDTH)


def _post_kernel(x_ref, a_ref, bo_ref, g_ref, wga_ref, wgb_ref, bg_ref, wa_ref, wb_ref, wo_ref,
                 out_ref, mix_ref):
    x = x_ref[...]
    h = _rms(x, g_ref[...]).astype(BF16)
    a = a_ref[...]
    bo = bo_ref[...]
    cw = 256
    for c0 in range(0, D_MODEL, cw):
        cs = slice(c0, c0 + cw)
        ap = jnp.dot(a, wa_ref[:, cs], preferred_element_type=F32)
        bp = jnp.dot(bo, wb_ref[:, cs], preferred_element_type=F32)
        ga = jnp.dot(h, wga_ref[:, cs], preferred_element_type=F32) + bg_ref[:, cs]
        gb = jnp.dot(h, wgb_ref[:, cs], preferred_element_type=F32) + bg_ref[:, D_MODEL + c0:D_MODEL + c0 + cw]
        mix_ref[:, cs] = (jax.nn.sigmoid(ga) * ap + jax.nn.sigmoid(gb) * bp).astype(BF16)
    out_ref[...] = x + jnp.dot(mix_ref[...], wo_ref[...], preferred_element_type=F32)


def _post_call(x, a, bo, g, w_in, bg, wa, wb, wo):
    n = x.shape[0]
    t = ROW_TILE
    rows = lambda w: pl.BlockSpec((t, w), lambda i: (i, 0))
    gate_blk = GATE_OFF // D_MODEL
    return pl.pallas_call(
        _post_kernel,
        grid=(n // t,),
        in_specs=[rows(D_MODEL), rows(A_GROUP_WIDTH), rows(B_Q_WIDTH), _const_spec(g.shape),
                  _const_spec((D_MODEL, D_MODEL), (0, gate_blk)),
                  _const_spec((D_MODEL, D_MODEL), (0, gate_blk + 1)),
                  _const_spec(bg.shape), _const_spec(wa.shape), _const_spec(wb.shape), _const_spec(wo.shape)],
        out_specs=rows(D_MODEL),
        out_shape=jax.ShapeDtypeStruct((n, D_MODEL), F32),
        scratch_shapes=[pltpu.VMEM((t, D_MODEL), BF16)],
        compiler_params=_params(1),
        name="post_attn",
    )(x, a, bo, g, w_in, w_in, bg, wa, wb, wo)


def _mlp_kernel(xp_ref, x_ref, xn_ref, g_ref, wup_ref, cw_ref, cb_ref, wdn_ref, gf_ref, out_ref,
                act_ref, *, tiles_per_seq):
    t = x_ref.shape[0]
    i = pl.program_id(0)
    first = (i % tiles_per_seq) == 0
    last = (i % tiles_per_seq) == tiles_per_seq - 1
    x = x_ref[...]
    xe = jnp.concatenate([xp_ref[...], x, xn_ref[...]], axis=0)
    he = _rms(xe, g_ref[...])
    rid = lax.broadcasted_iota(jnp.int32, (t + 2 * HALO, 1), 0)
    row_lo = jnp.where(first, HALO, 0)
    row_hi = jnp.where(last, t + HALO, t + 2 * HALO)
    he = jnp.where((rid >= row_lo) & (rid < row_hi), he, 0.0).astype(BF16)

    def conv(u, c0):
        w = cw_ref[:, c0:c0 + FF_CHUNK]
        return (u[HALO - 1:HALO - 1 + t] * w[0:1] + u[HALO:HALO + t] * w[1:2]
                + u[HALO + 1:HALO + 1 + t] * w[2:3] + cb_ref[:, c0:c0 + FF_CHUNK])

    for c0 in range(0, D_FF, FF_CHUNK):
        ug = jnp.dot(he, wup_ref[:, c0:c0 + FF_CHUNK], preferred_element_type=F32)
        uv = jnp.dot(he, wup_ref[:, D_FF + c0:D_FF + c0 + FF_CHUNK], preferred_element_type=F32)
        act_ref[:, c0:c0 + FF_CHUNK] = (jax.nn.gelu(conv(ug, c0)) * conv(uv, D_FF + c0)).astype(BF16)
    y = x + jnp.dot(act_ref[...], wdn_ref[...], preferred_element_type=F32)
    out_ref[...] = _rms(y, gf_ref[...])


def _mlp_call(x, s, g, wup, cw, cb, wdn, gf):
    n = x.shape[0]
    t = ROW_TILE
    hb = t // HALO
    nhb = n // HALO
    consts = (g, wup, cw, cb, wdn, gf)
    return pl.pallas_call(
        functools.partial(_mlp_kernel, tiles_per_seq=s // t),
        grid=(n // t,),
        in_specs=[pl.BlockSpec((HALO, D_MODEL), lambda i: (jnp.maximum(i * hb - 1, 0), 0)),
                  pl.BlockSpec((t, D_MODEL), lambda i: (i, 0)),
                  pl.BlockSpec((HALO, D_MODEL), lambda i: (jnp.minimum((i + 1) * hb, nhb - 1), 0))]
        + [_const_spec(c.shape) for c in consts],
        out_specs=pl.BlockSpec((t, D_MODEL), lambda i: (i, 0)),
        out_shape=jax.ShapeDtypeStruct((n, D_MODEL), F32),
        scratch_shapes=[pltpu.VMEM((t, D_FF), BF16)],
        compiler_params=_params(1),
        name="conv_mlp",
    )(x, x, x, *consts)


def _rel_bucket(rel):
    nb = N_BUCKETS // 2
    max_exact = nb // 2
    rel = np.asarray(rel, np.int32)
    ret = np.where(rel > 0, nb, 0)
    n = np.abs(rel)
    nf = np.maximum(n, 1).astype(np.float32)
    large = max_exact + (np.log(nf / np.float32(max_exact)) / np.float32(math.log(MAX_DISTANCE / max_exact))
                         * np.float32(nb - max_exact)).astype(np.int32)
    large = np.minimum(large, nb - 1)
    return (ret + np.where(n < max_exact, n, large)).astype(np.int32)


def _bias_tiles(rel_bias, head0, n_heads, dilation, win, side, shifts):
    offs = np.arange(-side, side + 1) * dilation
    line = rel_bias[_rel_bucket(offs)][:, head0:head0 + n_heads].T.astype(F32)
    k0 = QBLK + max(shifts) - side
    p = win + k0 + side + 1
    padded = jnp.pad(line, ((0, 0), (k0, p - k0 - (2 * side + 1))), constant_values=NEG)
    skew = jnp.tile(padded, (1, QBLK + 1))[:, :QBLK * (p - 1)].reshape(n_heads, QBLK, p - 1)
    return jnp.stack([skew[:, :, k0 - sh + side:k0 - sh + side + win] for sh in shifts], 0)


def _prep(w_in, b_gate, rel_bias, sink, w_a_out, w_b_out, w_o, w_up, conv_w, conv_b, w_down,
          g_attn, g_ffn, g_final):
    row = lambda v: v.reshape(1, -1).astype(F32)
    p = dict(
        w_in=w_in.astype(BF16), b_gate=row(b_gate),
        w_a_out=w_a_out.astype(BF16), w_b_out=w_b_out.astype(BF16), w_o=w_o.astype(BF16),
        w_up=w_up.astype(BF16), conv_w=conv_w.astype(F32), conv_b=row(conv_b),
        w_down=w_down.astype(BF16), g_attn=row(g_attn), g_ffn=row(g_ffn), g_final=row(g_final),
        rel_bias=rel_bias,
    )
    bb = _bias_tiles(rel_bias, A_HEADS, B_Q_HEADS, 1, B_WIN, B_RADIUS, (0, B_RADIUS, 2 * B_RADIUS))
    p["bias_b"] = bb.reshape(3, B_KV_HEADS, B_GROUP * QBLK, B_WIN).transpose(1, 0, 2, 3)
    p["sink"] = sink.astype(F32)
    return p


def _group_biases(rel_bias, s):
    out = []
    for gi, d in enumerate(DILATIONS):
        m = s // d
        win = min(A_WIN, m)
        shifts = (0,) if m == QBLK else (0, A_SIDE, 2 * A_SIDE)
        t = _bias_tiles(rel_bias, gi * A_HEADS_PER_GROUP, A_HEADS_PER_GROUP, d, win, A_SIDE, shifts)
        out.append(t.reshape(len(shifts), 2, 2 * QBLK, win))
    return out


def _trunk(x, p):
    b, s, _ = x.shape
    xf = x.reshape(b * s, D_MODEL)
    g1, g2, g3, qb = _qkv_call(xf, p["g_attn"], p["w_in"], b, s)
    a = _attn_a_call((g1, g2, g3), _group_biases(p["rel_bias"], s), b, s)
    bo = _attn_b_call(qb, p["bias_b"], p["sink"], b, s)
    x1 = _post_call(xf, a, bo, p["g_attn"], p["w_in"], p["b_gate"], p["w_a_out"], p["w_b_out"], p["w_o"])
    y = _mlp_call(x1, s, p["g_ffn"], p["w_up"], p["conv_w"], p["conv_b"], p["w_down"], p["g_final"])
    return y.reshape(b, s, D_MODEL)


def kernel(x_prompt, x_sample, g_attn, w_in, b_gate, rel_bias, sink, w_a_out, w_b_out, w_o, g_ffn,
           w_up, conv_w, conv_b, w_down, g_final):
    assert w_in.shape[0] == 1, "one layer: the final RMSNorm is fused into its MLP kernel"
    p = _prep(w_in[0], b_gate[0], rel_bias, sink[0], w_a_out[0], w_b_out[0], w_o[0], w_up[0],
              conv_w[0], conv_b[0], w_down[0], g_attn[0], g_ffn[0], g_final)
    return (_trunk(x_prompt, p), _trunk(x_sample, p))
```

```python
import functools
import math

import numpy as np
import jax
import jax.numpy as jnp
from jax import lax
from jax.experimental import pallas as pl
from jax.experimental.pallas import tpu as pltpu

F32 = jnp.float32
BF16 = jnp.bfloat16

D_MODEL = 1024
HEAD_DIM = 64
DIL_PATTERNS = ((128, 1), (512, 4), (2048, 16))
DILATIONS = tuple(d for _, d in DIL_PATTERNS)
A_HEADS_PER_GROUP = 4
A_GROUP_WIDTH = A_HEADS_PER_GROUP * HEAD_DIM
N_GROUPS = len(DIL_PATTERNS)
A_HEADS = N_GROUPS * A_HEADS_PER_GROUP
A_WIDTH = A_HEADS * HEAD_DIM
A_SIDE = 64
B_Q_HEADS = 8
B_KV_HEADS = 2
B_GROUP = B_Q_HEADS // B_KV_HEADS
B_Q_WIDTH = B_Q_HEADS * HEAD_DIM
B_KV_WIDTH = B_KV_HEADS * HEAD_DIM
B_RADIUS = 128
N_BUCKETS = 32
MAX_DISTANCE = 1024
D_FF = 2816
EPS = 1e-6
NEG = -1e30
LOG2E = math.log2(math.e)
Q_SCALE = HEAD_DIM ** -0.5 * LOG2E

KA_OFF, VA_OFF = A_WIDTH, 2 * A_WIDTH
QB_OFF = 3 * A_WIDTH
KVB_OFF = QB_OFF + B_Q_WIDTH
GATE_OFF = KVB_OFF + 2 * B_KV_WIDTH
QKV_COLS = GATE_OFF

PAIR = 2 * HEAD_DIM
PAIR_COLS = 3 * PAIR
GRP_COLS = 2 * PAIR_COLS
B_COLS = B_Q_WIDTH + 4 * PAIR
QBLK = 128
A_WIN = QBLK + 2 * A_SIDE
B_WIN = QBLK + 2 * B_RADIUS
ROW_TILE = 512
FF_CHUNK = 256
HALO = 8
VMEM_LIMIT = 56 * 1024 * 1024


def _params(n_axes):
    return pltpu.CompilerParams(dimension_semantics=("arbitrary",) * n_axes,
                                vmem_limit_bytes=VMEM_LIMIT)


def _const_spec(shape, index=None):
    index = (0,) * len(shape) if index is None else index
    return pl.BlockSpec(shape, lambda *_: index, pipeline_mode=pl.Buffered(1))


def _rms(x, g):
    ms = jnp.mean(x * x, axis=-1, keepdims=True)
    return x * lax.rsqrt(ms + EPS) * g


def _lane_lo(rows=QBLK):
    return lax.broadcasted_iota(jnp.int32, (rows, PAIR), 1) < HEAD_DIM


def _pair_rhs(v):
    lo = _lane_lo(v.shape[0])
    zero = jnp.zeros_like(v)
    ones_lo = jnp.where(lo, 1.0, 0.0).astype(v.dtype)
    ones_hi = jnp.where(lo, 0.0, 1.0).astype(v.dtype)
    top = jnp.concatenate([jnp.where(lo, v, zero), ones_lo], axis=1)
    bot = jnp.concatenate([jnp.where(lo, zero, v), ones_hi], axis=1)
    return jnp.concatenate([top, bot], axis=0)


def _qkv_kernel(x_ref, g_ref, w_ref, o1_ref, o2_ref, o3_ref, ob_ref, tmp_ref):
    t = x_ref.shape[0]
    h = _rms(x_ref[...], g_ref[...]).astype(BF16)

    def proj(c0, n):
        return jnp.dot(h, w_ref[:, c0:c0 + n], preferred_element_type=F32)

    for gi, (o_ref, d) in enumerate(zip((o1_ref, o2_ref, o3_ref), DILATIONS)):
        c0 = gi * A_GROUP_WIDTH
        parts = (proj(c0, A_GROUP_WIDTH) * Q_SCALE, proj(KA_OFF + c0, A_GROUP_WIDTH),
                 proj(VA_OFF + c0, A_GROUP_WIDTH))
        for p in range(2):
            for j, part in enumerate(parts):
                c = p * 3 + j
                val = part[:, p * PAIR:(p + 1) * PAIR]
                if d == 1:
                    o_ref[0, :, c * PAIR:(c + 1) * PAIR] = val.astype(BF16)
                else:
                    tmp_ref[c] = val
        if d > 1:
            for r in range(d):
                for c in range(GRP_COLS // PAIR):
                    o_ref[r, :, c * PAIR:(c + 1) * PAIR] = (
                        tmp_ref[c, pl.ds(r, t // d, stride=d), :].astype(BF16))

    ob_ref[:, :B_Q_WIDTH] = (proj(QB_OFF, B_Q_WIDTH) * Q_SCALE).astype(BF16)
    kv = proj(KVB_OFF, 2 * B_KV_WIDTH)
    lo = lax.broadcasted_iota(jnp.int32, (t, PAIR), 1) < HEAD_DIM
    for j in range(2):
        one = kv[:, j * PAIR:(j + 1) * PAIR]
        swapped = pltpu.roll(one, HEAD_DIM, axis=1)
        c0 = B_Q_WIDTH + j * 2 * PAIR
        ob_ref[:, c0:c0 + PAIR] = jnp.where(lo, one, swapped).astype(BF16)
        ob_ref[:, c0 + PAIR:c0 + 2 * PAIR] = jnp.where(lo, swapped, one).astype(BF16)


def _qkv_call(x, g, w_in, b, s):
    n = x.shape[0]
    t = ROW_TILE
    tps = s // t
    grp_shape = lambda d: jax.ShapeDtypeStruct((b, d, s // d, GRP_COLS), BF16)
    grp_spec = lambda d: pl.BlockSpec((None, d, t // d, GRP_COLS), lambda i: (i // tps, 0, i % tps, 0))
    return pl.pallas_call(
        _qkv_kernel,
        grid=(n // t,),
        in_specs=[pl.BlockSpec((t, D_MODEL), lambda i: (i, 0)), _const_spec(g.shape),
                  _const_spec((D_MODEL, QKV_COLS))],
        out_specs=[grp_spec(d) for d in DILATIONS] + [pl.BlockSpec((t, B_COLS), lambda i: (i, 0))],
        out_shape=[grp_shape(d) for d in DILATIONS] + [jax.ShapeDtypeStruct((n, B_COLS), BF16)],
        scratch_shapes=[pltpu.VMEM((GRP_COLS // PAIR, t, PAIR), F32)],
        compiler_params=_params(1),
        name="qkv_proj",
    )(x, g, w_in)


def _window(i, nblk, m, win, side):
    if nblk == 1:
        return 0, 0
    start = pl.multiple_of(jnp.clip(i * QBLK - side, 0, m - win), 64)
    var = jnp.where(i == 0, 0, jnp.where(i == nblk - 1, 2, 1))
    return start, var


def _attn_a_kernel(g1_ref, g2_ref, g3_ref, b1_ref, b2_ref, b3_ref, out_ref, acc_ref, m_ref, l_ref, *, s):
    lo = _lane_lo()
    hi = jnp.logical_not(lo)
    zero = jnp.zeros((QBLK, PAIR), BF16)

    def attend(q, k, v, bias):
        qs = jnp.concatenate([jnp.where(lo, q, zero), jnp.where(hi, q, zero)], axis=0)
        sc = lax.dot_general(qs, k, (((1,), (1,)), ((), ())), preferred_element_type=F32) + bias
        mx = jnp.max(sc, axis=-1, keepdims=True)
        e = jnp.exp2(sc - mx).astype(BF16)
        r = jnp.dot(jnp.concatenate([e[:QBLK], e[QBLK:]], axis=1), _pair_rhs(v), preferred_element_type=F32)
        return r[:, :PAIR], jnp.where(lo, mx[:QBLK], mx[QBLK:]), r[:, PAIR:]

    for gi, (ref, bias_ref, d) in enumerate(zip((g1_ref, g2_ref, g3_ref), (b1_ref, b2_ref, b3_ref),
                                                DILATIONS)):
        m = s // d
        nblk = m // QBLK
        win = min(A_WIN, m)
        shift = nblk.bit_length() - 1
        assert nblk == 1 << shift

        def body(j, carry, ref=ref, bias_ref=bias_ref, d=d, m=m, nblk=nblk, win=win, shift=shift, gi=gi):
            r, i = (0, j) if d == 1 else (j >> shift, j & (nblk - 1))
            q0 = pl.multiple_of(i * QBLK, QBLK)
            start, var = _window(i, nblk, m, win, A_SIDE)
            o, mm, ll = attend(ref[r, pl.ds(q0, QBLK), :PAIR], ref[r, pl.ds(start, win), PAIR:2 * PAIR],
                               ref[r, pl.ds(start, win), 2 * PAIR:], bias_ref[var])
            rows = pl.ds(q0, QBLK) if d == 1 else pl.ds(r + d * q0, QBLK, stride=d)
            if gi == 0:
                acc_ref[rows, :] = o
                m_ref[rows, :] = mm
                l_ref[rows, :] = ll
            else:
                m_old = m_ref[rows, :]
                m_new = jnp.maximum(m_old, mm)
                alpha = jnp.exp2(m_old - m_new)
                beta = jnp.exp2(mm - m_new)
                m_ref[rows, :] = m_new
                l_ref[rows, :] = alpha * l_ref[rows, :] + beta * ll
                acc_ref[rows, :] = alpha * acc_ref[rows, :] + beta * o
            return carry

        lax.fori_loop(0, d * nblk, body, 0, unroll=8)

    out_ref[...] = (acc_ref[...] / l_ref[...]).astype(BF16)


def _attn_a_call(groups, biases, b, s):
    in_specs = [pl.BlockSpec((None, d, s // d, PAIR_COLS), lambda bi, p: (bi, 0, 0, p)) for d in DILATIONS]
    in_specs += [pl.BlockSpec((bias.shape[0], None) + bias.shape[2:], lambda bi, p: (0, p, 0, 0))
                 for bias in biases]
    out = pl.pallas_call(
        functools.partial(_attn_a_kernel, s=s),
        grid=(b, 2),
        in_specs=in_specs,
        out_specs=pl.BlockSpec((None, s, PAIR), lambda bi, p: (bi, 0, p)),
        out_shape=jax.ShapeDtypeStruct((b, s, A_GROUP_WIDTH), BF16),
        scratch_shapes=[pltpu.VMEM((s, PAIR), F32)] * 3,
        compiler_params=_params(2),
        name="attn_dil",
    )(*groups, *biases)
    return out.reshape(b * s, A_GROUP_WIDTH)


def _attn_b_kernel(q_ref, k_ref, v_ref, bias_ref, sink_ref, o_ref, *, s):
    nblk = s // QBLK
    lo = _lane_lo()
    hi = jnp.logical_not(lo)
    zero = jnp.zeros((QBLK, PAIR), BF16)
    c = pl.program_id(1)

    def block(i, carry):
        q0 = pl.multiple_of(i * QBLK, QBLK)
        start, var = _window(i, nblk, s, B_WIN, B_RADIUS)
        kw = k_ref[pl.ds(start, B_WIN), :]
        rhs = _pair_rhs(v_ref[pl.ds(start, B_WIN), :])
        qs = []
        for p in range(2):
            q = q_ref[pl.ds(q0, QBLK), p * PAIR:(p + 1) * PAIR]
            qs += [jnp.where(lo, q, zero), jnp.where(hi, q, zero)]
        sc = lax.dot_general(jnp.concatenate(qs, axis=0), kw, (((1,), (1,)), ((), ())),
                             preferred_element_type=F32) + bias_ref[var]
        es, sinks = [], []
        for p in range(2):
            sk_e = sink_ref[c * B_GROUP + 2 * p]
            sk_o = sink_ref[c * B_GROUP + 2 * p + 1]
            sp = sc[p * 2 * QBLK:(p + 1) * 2 * QBLK]
            mx = jnp.max(sp, axis=-1, keepdims=True)
            mx = jnp.concatenate([jnp.maximum(mx[:QBLK], sk_e), jnp.maximum(mx[QBLK:], sk_o)], axis=0)
            e = jnp.exp2(sp - mx).astype(BF16)
            es.append(jnp.concatenate([e[:QBLK], e[QBLK:]], axis=1))
            sinks.append(jnp.where(lo, jnp.exp2(sk_e - mx[:QBLK]), jnp.exp2(sk_o - mx[QBLK:])))
        r = jnp.dot(jnp.concatenate(es, axis=0), rhs, preferred_element_type=F32)
        for p in range(2):
            rp = r[p * QBLK:(p + 1) * QBLK]
            o_ref[pl.ds(q0, QBLK), p * PAIR:(p + 1) * PAIR] = (rp[:, :PAIR] / (rp[:, PAIR:] + sinks[p])).astype(BF16)
        return carry

    lax.fori_loop(0, nblk, block, 0, unroll=4)


def _attn_b_call(qkv_b, bias, sink, b, s):
    qkv_v = qkv_b.reshape(b, s, B_COLS)
    qw = B_GROUP * HEAD_DIM
    kv0 = B_Q_WIDTH // PAIR
    o = pl.pallas_call(
        functools.partial(_attn_b_kernel, s=s),
        grid=(b, B_KV_HEADS),
        in_specs=[pl.BlockSpec((None, s, qw), lambda bi, c: (bi, 0, c)),
                  pl.BlockSpec((None, s, PAIR), lambda bi, c: (bi, 0, kv0 + c)),
                  pl.BlockSpec((None, s, PAIR), lambda bi, c: (bi, 0, kv0 + B_KV_HEADS + c)),
                  pl.BlockSpec((None,) + bias.shape[1:], lambda bi, c: (c, 0, 0, 0)),
                  pl.BlockSpec(memory_space=pltpu.SMEM)],
        out_specs=pl.BlockSpec((None, s, qw), lambda bi, c: (bi, 0, c)),
        out_shape=jax.ShapeDtypeStruct((b, s, B_Q_WIDTH), BF16),
        compiler_params=_params(2),
        name="attn_win",
    )(qkv_v, qkv_v, qkv_v, bias, sink)
    return o.reshape(b * s, B_Q_WIDTH)


def _post_kernel(x_ref, a_ref, bo_ref, g_ref, wga_ref, wgb_ref, bg_ref, wa_ref, wb_ref, wo_ref,
                 out_ref, mix_ref):
    x = x_ref[...]
    h = _rms(x, g_ref[...]).astype(BF16)
    a = a_ref[...]
    bo = bo_ref[...]
    cw = 256
    for c0 in range(0, D_MODEL, cw):
        cs = slice(c0, c0 + cw)
        ap = jnp.dot(a, wa_ref[:, cs], preferred_element_type=F32)
        bp = jnp.dot(bo, wb_ref[:, cs], preferred_element_type=F32)
        ga = jnp.dot(h, wga_ref[:, cs], preferred_element_type=F32) + bg_ref[:, cs]
        gb = jnp.dot(h, wgb_ref[:, cs], preferred_element_type=F32) + bg_ref[:, D_MODEL + c0:D_MODEL + c0 + cw]
        mix_ref[:, cs] = (jax.nn.sigmoid(ga) * ap + jax.nn.sigmoid(gb) * bp).astype(BF16)
    out_ref[...] = x + jnp.dot(mix_ref[...], wo_ref[...], preferred_element_type=F32)


def _post_call(x, a, bo, g, w_in, bg, wa, wb, wo):
    n = x.shape[0]
    t = ROW_TILE
    rows = lambda w: pl.BlockSpec((t, w), lambda i: (i, 0))
    gate_blk = GATE_OFF // D_MODEL
    return pl.pallas_call(
        _post_kernel,
        grid=(n // t,),
        in_specs=[rows(D_MODEL), rows(A_GROUP_WIDTH), rows(B_Q_WIDTH), _const_spec(g.shape),
                  _const_spec((D_MODEL, D_MODEL), (0, gate_blk)),
                  _const_spec((D_MODEL, D_MODEL), (0, gate_blk + 1)),
                  _const_spec(bg.shape), _const_spec(wa.shape), _const_spec(wb.shape), _const_spec(wo.shape)],
        out_specs=rows(D_MODEL),
        out_shape=jax.ShapeDtypeStruct((n, D_MODEL), F32),
        scratch_shapes=[pltpu.VMEM((t, D_MODEL), BF16)],
        compiler_params=_params(1),
        name="post_attn",
    )(x, a, bo, g, w_in, w_in, bg, wa, wb, wo)


def _mlp_kernel(xp_ref, x_ref, xn_ref, g_ref, wup_ref, cw_ref, cb_ref, wdn_ref, gf_ref, out_ref,
                act_ref, *, tiles_per_seq):
    t = x_ref.shape[0]
    i = pl.program_id(0)
    first = (i % tiles_per_seq) == 0
    last = (i % tiles_per_seq) == tiles_per_seq - 1
    x = x_ref[...]
    xe = jnp.concatenate([xp_ref[...], x, xn_ref[...]], axis=0)
    he = _rms(xe, g_ref[...])
    rid = lax.broadcasted_iota(jnp.int32, (t + 2 * HALO, 1), 0)
    row_lo = jnp.where(first, HALO, 0)
    row_hi = jnp.where(last, t + HALO, t + 2 * HALO)
    he = jnp.where((rid >= row_lo) & (rid < row_hi), he, 0.0).astype(BF16)

    def conv(u, c0):
        w = cw_ref[:, c0:c0 + FF_CHUNK]
        return (u[HALO - 1:HALO - 1 + t] * w[0:1] + u[HALO:HALO + t] * w[1:2]
                + u[HALO + 1:HALO + 1 + t] * w[2:3] + cb_ref[:, c0:c0 + FF_CHUNK])

    for c0 in range(0, D_FF, FF_CHUNK):
        ug = jnp.dot(he, wup_ref[:, c0:c0 + FF_CHUNK], preferred_element_type=F32)
        uv = jnp.dot(he, wup_ref[:, D_FF + c0:D_FF + c0 + FF_CHUNK], preferred_element_type=F32)
        act_ref[:, c0:c0 + FF_CHUNK] = (jax.nn.gelu(conv(ug, c0)) * conv(uv, D_FF + c0)).astype(BF16)
    y = x + jnp.dot(act_ref[...], wdn_ref[...], preferred_element_type=F32)
    out_ref[...] = _rms(y, gf_ref[...])


def _mlp_call(x, s, g, wup, cw, cb, wdn, gf):
    n = x.shape[0]
    t = ROW_TILE
    hb = t // HALO
    nhb = n // HALO
    consts = (g, wup, cw, cb, wdn, gf)
    return pl.pallas_call(
        functools.partial(_mlp_kernel, tiles_per_seq=s // t),
        grid=(n // t,),
        in_specs=[pl.BlockSpec((HALO, D_MODEL), lambda i: (jnp.maximum(i * hb - 1, 0), 0)),
                  pl.BlockSpec((t, D_MODEL), lambda i: (i, 0)),
                  pl.BlockSpec((HALO, D_MODEL), lambda i: (jnp.minimum((i + 1) * hb, nhb - 1), 0))]
        + [_const_spec(c.shape) for c in consts],
        out_specs=pl.BlockSpec((t, D_MODEL), lambda i: (i, 0)),
        out_shape=jax.ShapeDtypeStruct((n, D_MODEL), F32),
        scratch_shapes=[pltpu.VMEM((t, D_FF), BF16)],
        compiler_params=_params(1),
        name="conv_mlp",
    )(x, x, x, *consts)


def _rel_bucket(rel):
    nb = N_BUCKETS // 2
    max_exact = nb // 2
    rel = np.asarray(rel, np.int32)
    ret = np.where(rel > 0, nb, 0)
    n = np.abs(rel)
    nf = np.maximum(n, 1).astype(np.float32)
    large = max_exact + (np.log(nf / np.float32(max_exact)) / np.float32(math.log(MAX_DISTANCE / max_exact))
                         * np.float32(nb - max_exact)).astype(np.int32)
    large = np.minimum(large, nb - 1)
    return (ret + np.where(n < max_exact, n, large)).astype(np.int32)


def _bias_tiles(rel_bias, head0, n_heads, dilation, win, side, shifts):
    offs = np.arange(-side, side + 1) * dilation
    line = rel_bias[_rel_bucket(offs)][:, head0:head0 + n_heads].T.astype(F32) * LOG2E
    k0 = QBLK + max(shifts) - side
    p = win + k0 + side + 1
    padded = jnp.pad(line, ((0, 0), (k0, p - k0 - (2 * side + 1))), constant_values=NEG)
    skew = jnp.tile(padded, (1, QBLK + 1))[:, :QBLK * (p - 1)].reshape(n_heads, QBLK, p - 1)
    return jnp.stack([skew[:, :, k0 - sh + side:k0 - sh + side + win] for sh in shifts], 0)


def _prep(w_in, b_gate, rel_bias, sink, w_a_out, w_b_out, w_o, w_up, conv_w, conv_b, w_down,
          g_attn, g_ffn, g_final):
    row = lambda v: v.reshape(1, -1).astype(F32)
    p = dict(
        w_in=w_in.astype(BF16), b_gate=row(b_gate),
        w_a_out=w_a_out.astype(BF16), w_b_out=w_b_out.astype(BF16), w_o=w_o.astype(BF16),
        w_up=w_up.astype(BF16), conv_w=conv_w.astype(F32), conv_b=row(conv_b),
        w_down=w_down.astype(BF16), g_attn=row(g_attn), g_ffn=row(g_ffn), g_final=row(g_final),
        rel_bias=rel_bias,
    )
    bb = _bias_tiles(rel_bias, A_HEADS, B_Q_HEADS, 1, B_WIN, B_RADIUS, (0, B_RADIUS, 2 * B_RADIUS))
    p["bias_b"] = bb.reshape(3, B_KV_HEADS, B_GROUP * QBLK, B_WIN).transpose(1, 0, 2, 3)
    p["sink"] = sink.astype(F32) * LOG2E
    return p


def _group_biases(rel_bias, s):
    out = []
    for gi, d in enumerate(DILATIONS):
        m = s // d
        win = min(A_WIN, m)
        shifts = (0,) if m == QBLK else (0, A_SIDE, 2 * A_SIDE)
        t = _bias_tiles(rel_bias, gi * A_HEADS_PER_GROUP, A_HEADS_PER_GROUP, d, win, A_SIDE, shifts)
        out.append(t.reshape(len(shifts), 2, 2 * QBLK, win))
    return out


def _trunk(x, p):
    b, s, _ = x.shape
    xf = x.reshape(b * s, D_MODEL)
    g1, g2, g3, qb = _qkv_call(xf, p["g_attn"], p["w_in"], b, s)
    a = _attn_a_call((g1, g2, g3), _group_biases(p["rel_bias"], s), b, s)
    bo = _attn_b_call(qb, p["bias_b"], p["sink"], b, s)
    x1 = _post_call(xf, a, bo, p["g_attn"], p["w_in"], p["b_gate"], p["w_a_out"], p["w_b_out"], p["w_o"])
    y = _mlp_call(x1, s, p["g_ffn"], p["w_up"], p["conv_w"], p["conv_b"], p["w_down"], p["g_final"])
    return y.reshape(b, s, D_MODEL)


def kernel(x_prompt, x_sample, g_attn, w_in, b_gate, rel_bias, sink, w_a_out, w_b_out, w_o, g_ffn,
           w_up, conv_w, conv_b, w_down, g_final):
    assert w_in.shape[0] == 1, "one layer: the final RMSNorm is fused into its MLP kernel"
    p = _prep(w_in[0], b_gate[0], rel_bias, sink[0], w_a_out[0], w_b_out[0], w_o[0], w_up[0],
              conv_w[0], conv_b[0], w_down[0], g_attn[0], g_ffn[0], g_final)
    return (_trunk(x_prompt, p), _trunk(x_sample, p))
```

```python
import functools
import math

import numpy as np
import jax
import jax.numpy as jnp
from jax import lax
from jax.experimental import pallas as pl
from jax.experimental.pallas import tpu as pltpu

F32 = jnp.float32
BF16 = jnp.bfloat16

D_MODEL = 1024
HEAD_DIM = 64
DIL_PATTERNS = ((128, 1), (512, 4), (2048, 16))
DILATIONS = tuple(d for _, d in DIL_PATTERNS)
A_HEADS_PER_GROUP = 4
A_GROUP_WIDTH = A_HEADS_PER_GROUP * HEAD_DIM
N_GROUPS = len(DIL_PATTERNS)
A_HEADS = N_GROUPS * A_HEADS_PER_GROUP
A_WIDTH = A_HEADS * HEAD_DIM
A_SIDE = 64
B_Q_HEADS = 8
B_KV_HEADS = 2
B_GROUP = B_Q_HEADS // B_KV_HEADS
B_Q_WIDTH = B_Q_HEADS * HEAD_DIM
B_KV_WIDTH = B_KV_HEADS * HEAD_DIM
B_RADIUS = 128
N_BUCKETS = 32
MAX_DISTANCE = 1024
D_FF = 2816
EPS = 1e-6
NEG = -1e30
LOG2E = math.log2(math.e)
Q_SCALE = HEAD_DIM ** -0.5 * LOG2E

KA_OFF, VA_OFF = A_WIDTH, 2 * A_WIDTH
QB_OFF = 3 * A_WIDTH
KVB_OFF = QB_OFF + B_Q_WIDTH
GATE_OFF = KVB_OFF + 2 * B_KV_WIDTH
QKV_COLS = GATE_OFF

PAIR = 2 * HEAD_DIM
PAIR_COLS = 3 * PAIR
GRP_COLS = 2 * PAIR_COLS
B_COLS = B_Q_WIDTH + 4 * PAIR
QBLK = 128
A_WIN = QBLK + 2 * A_SIDE
B_WIN = QBLK + 2 * B_RADIUS
ROW_TILE = 512
FF_CHUNK = 256
HALO = 8
VMEM_LIMIT = 56 * 1024 * 1024


def _params(n_axes):
    return pltpu.CompilerParams(dimension_semantics=("arbitrary",) * n_axes,
                                vmem_limit_bytes=VMEM_LIMIT)


def _const_spec(shape, index=None):
    index = (0,) * len(shape) if index is None else index
    return pl.BlockSpec(shape, lambda *_: index, pipeline_mode=pl.Buffered(1))


def _rms(x, g):
    ms = jnp.mean(x * x, axis=-1, keepdims=True)
    return x * lax.rsqrt(ms + EPS) * g


def _lane_lo(rows=QBLK):
    return lax.broadcasted_iota(jnp.int32, (rows, PAIR), 1) < HEAD_DIM


def _pair_rhs(v):
    lo = _lane_lo(v.shape[0])
    zero = jnp.zeros_like(v)
    ones_lo = jnp.where(lo, 1.0, 0.0).astype(v.dtype)
    ones_hi = jnp.where(lo, 0.0, 1.0).astype(v.dtype)
    top = jnp.concatenate([jnp.where(lo, v, zero), ones_lo], axis=1)
    bot = jnp.concatenate([jnp.where(lo, zero, v), ones_hi], axis=1)
    return jnp.concatenate([top, bot], axis=0)


def _qkv_kernel(x_ref, g_ref, w_ref, o1_ref, o2_ref, o3_ref, ob_ref, tmp_ref):
    t = x_ref.shape[0]
    h = _rms(x_ref[...], g_ref[...]).astype(BF16)

    def proj(c0, n):
        return jnp.dot(h, w_ref[:, c0:c0 + n], preferred_element_type=F32)

    for gi, (o_ref, d) in enumerate(zip((o1_ref, o2_ref, o3_ref), DILATIONS)):
        c0 = gi * A_GROUP_WIDTH
        parts = (proj(c0, A_GROUP_WIDTH) * Q_SCALE, proj(KA_OFF + c0, A_GROUP_WIDTH),
                 proj(VA_OFF + c0, A_GROUP_WIDTH))
        for p in range(2):
            for j, part in enumerate(parts):
                c = p * 3 + j
                val = part[:, p * PAIR:(p + 1) * PAIR]
                if d == 1:
                    o_ref[0, :, c * PAIR:(c + 1) * PAIR] = val.astype(BF16)
                else:
                    tmp_ref[c] = val
        if d > 1:
            for r in range(d):
                for c in range(GRP_COLS // PAIR):
                    o_ref[r, :, c * PAIR:(c + 1) * PAIR] = (
                        tmp_ref[c, pl.ds(r, t // d, stride=d), :].astype(BF16))

    ob_ref[:, :B_Q_WIDTH] = (proj(QB_OFF, B_Q_WIDTH) * Q_SCALE).astype(BF16)
    kv = proj(KVB_OFF, 2 * B_KV_WIDTH)
    lo = lax.broadcasted_iota(jnp.int32, (t, PAIR), 1) < HEAD_DIM
    for j in range(2):
        one = kv[:, j * PAIR:(j + 1) * PAIR]
        swapped = pltpu.roll(one, HEAD_DIM, axis=1)
        c0 = B_Q_WIDTH + j * 2 * PAIR
        ob_ref[:, c0:c0 + PAIR] = jnp.where(lo, one, swapped).astype(BF16)
        ob_ref[:, c0 + PAIR:c0 + 2 * PAIR] = jnp.where(lo, swapped, one).astype(BF16)


def _qkv_call(x, g, w_in, b, s):
    n = x.shape[0]
    t = ROW_TILE
    tps = s // t
    grp_shape = lambda d: jax.ShapeDtypeStruct((b, d, s // d, GRP_COLS), BF16)
    grp_spec = lambda d: pl.BlockSpec((None, d, t // d, GRP_COLS), lambda i: (i // tps, 0, i % tps, 0))
    return pl.pallas_call(
        _qkv_kernel,
        grid=(n // t,),
        in_specs=[pl.BlockSpec((t, D_MODEL), lambda i: (i, 0)), _const_spec(g.shape),
                  _const_spec((D_MODEL, QKV_COLS))],
        out_specs=[grp_spec(d) for d in DILATIONS] + [pl.BlockSpec((t, B_COLS), lambda i: (i, 0))],
        out_shape=[grp_shape(d) for d in DILATIONS] + [jax.ShapeDtypeStruct((n, B_COLS), BF16)],
        scratch_shapes=[pltpu.VMEM((GRP_COLS // PAIR, t, PAIR), F32)],
        compiler_params=_params(1),
        name="qkv_proj",
    )(x, g, w_in)


def _window(i, nblk, m, win, side):
    if nblk == 1:
        return 0, 0
    start = pl.multiple_of(jnp.clip(i * QBLK - side, 0, m - win), 64)
    var = jnp.where(i == 0, 0, jnp.where(i == nblk - 1, 2, 1))
    return start, var


def _attn_a_kernel(g1_ref, g2_ref, g3_ref, b1_ref, b2_ref, b3_ref, out_ref, acc_ref, m_ref, l_ref, *, s):
    lo = _lane_lo()
    hi = jnp.logical_not(lo)
    zero = jnp.zeros((QBLK, PAIR), BF16)

    def attend(q, k, v, bias):
        qs = jnp.concatenate([jnp.where(lo, q, zero), jnp.where(hi, q, zero)], axis=0)
        sc = lax.dot_general(qs, k, (((1,), (1,)), ((), ())), preferred_element_type=F32) + bias
        mx = jnp.max(sc, axis=-1, keepdims=True)
        e = jnp.exp2(sc - mx).astype(BF16)
        r = jnp.dot(jnp.concatenate([e[:QBLK], e[QBLK:]], axis=1), _pair_rhs(v), preferred_element_type=F32)
        return r[:, :PAIR], jnp.where(lo, mx[:QBLK], mx[QBLK:]), r[:, PAIR:]

    for gi, (ref, bias_ref, d) in enumerate(zip((g1_ref, g2_ref, g3_ref), (b1_ref, b2_ref, b3_ref),
                                                DILATIONS)):
        m = s // d
        nblk = m // QBLK
        win = min(A_WIN, m)
        shift = nblk.bit_length() - 1
        assert nblk == 1 << shift

        def body(j, carry, ref=ref, bias_ref=bias_ref, d=d, m=m, nblk=nblk, win=win, shift=shift, gi=gi):
            r, i = (0, j) if d == 1 else (j >> shift, j & (nblk - 1))
            q0 = pl.multiple_of(i * QBLK, QBLK)
            start, var = _window(i, nblk, m, win, A_SIDE)
            o, mm, ll = attend(ref[r, pl.ds(q0, QBLK), :PAIR], ref[r, pl.ds(start, win), PAIR:2 * PAIR],
                               ref[r, pl.ds(start, win), 2 * PAIR:], bias_ref[var])
            rows = pl.ds(q0, QBLK) if d == 1 else pl.ds(r + d * q0, QBLK, stride=d)
            if gi == 0:
                acc_ref[rows, :] = o
                m_ref[rows, :] = mm
                l_ref[rows, :] = ll
            else:
                m_old = m_ref[rows, :]
                m_new = jnp.maximum(m_old, mm)
                alpha = jnp.exp2(m_old - m_new)
                beta = jnp.exp2(mm - m_new)
                m_ref[rows, :] = m_new
                l_ref[rows, :] = alpha * l_ref[rows, :] + beta * ll
                acc_ref[rows, :] = alpha * acc_ref[rows, :] + beta * o
            return carry

        lax.fori_loop(0, d * nblk, body, 0, unroll=8)

    out_ref[...] = (acc_ref[...] / l_ref[...]).astype(BF16)


def _attn_a_call(groups, biases, b, s):
    in_specs = [pl.BlockSpec((None, d, s // d, PAIR_COLS), lambda bi, p: (bi, 0, 0, p)) for d in DILATIONS]
    in_specs += [pl.BlockSpec((bias.shape[0], None) + bias.shape[2:], lambda bi, p: (0, p, 0, 0))
                 for bias in biases]
    out = pl.pallas_call(
        functools.partial(_attn_a_kernel, s=s),
        grid=(b, 2),
        in_specs=in_specs,
        out_specs=pl.BlockSpec((None, s, PAIR), lambda bi, p: (bi, 0, p)),
        out_shape=jax.ShapeDtypeStruct((b, s, A_GROUP_WIDTH), BF16),
        scratch_shapes=[pltpu.VMEM((s, PAIR), F32)] * 3,
        compiler_params=_params(2),
        name="attn_dil",
    )(*groups, *biases)
    return out.reshape(b * s, A_GROUP_WIDTH)


def _attn_b_kernel(q_ref, k_ref, v_ref, bias_ref, sink_ref, o_ref, *, s):
    nblk = s // QBLK
    lo = _lane_lo()
    hi = jnp.logical_not(lo)
    zero = jnp.zeros((QBLK, PAIR), BF16)
    c = pl.program_id(1)

    def block(i, carry):
        q0 = pl.multiple_of(i * QBLK, QBLK)
        start, var = _window(i, nblk, s, B_WIN, B_RADIUS)
        kw = k_ref[pl.ds(start, B_WIN), :]
        rhs = _pair_rhs(v_ref[pl.ds(start, B_WIN), :])
        qs = []
        for p in range(2):
            q = q_ref[pl.ds(q0, QBLK), p * PAIR:(p + 1) * PAIR]
            qs += [jnp.where(lo, q, zero), jnp.where(hi, q, zero)]
        sc = lax.dot_general(jnp.concatenate(qs, axis=0), kw, (((1,), (1,)), ((), ())),
                             preferred_element_type=F32) + bias_ref[var]
        es, sinks = [], []
        for p in range(2):
            sk_e = sink_ref[c * B_GROUP + 2 * p]
            sk_o = sink_ref[c * B_GROUP + 2 * p + 1]
            sp = sc[p * 2 * QBLK:(p + 1) * 2 * QBLK]
            mx = jnp.max(sp, axis=-1, keepdims=True)
            mx = jnp.concatenate([jnp.maximum(mx[:QBLK], sk_e), jnp.maximum(mx[QBLK:], sk_o)], axis=0)
            e = jnp.exp2(sp - mx).astype(BF16)
            es.append(jnp.concatenate([e[:QBLK], e[QBLK:]], axis=1))
            sinks.append(jnp.where(lo, jnp.exp2(sk_e - mx[:QBLK]), jnp.exp2(sk_o - mx[QBLK:])))
        r = jnp.dot(jnp.concatenate(es, axis=0), rhs, preferred_element_type=F32)
        for p in range(2):
            rp = r[p * QBLK:(p + 1) * QBLK]
            o_ref[pl.ds(q0, QBLK), p * PAIR:(p + 1) * PAIR] = (rp[:, :PAIR] / (rp[:, PAIR:] + sinks[p])).astype(BF16)
        return carry

    lax.fori_loop(0, nblk, block, 0, unroll=4)


def _attn_b_call(qkv_b, bias, sink, b, s):
    qkv_v = qkv_b.reshape(b, s, B_COLS)
    qw = B_GROUP * HEAD_DIM
    kv0 = B_Q_WIDTH // PAIR
    o = pl.pallas_call(
        functools.partial(_attn_b_kernel, s=s),
        grid=(b, B_KV_HEADS),
        in_specs=[pl.BlockSpec((None, s, qw), lambda bi, c: (bi, 0, c)),
                  pl.BlockSpec((None, s, PAIR), lambda bi, c: (bi, 0, kv0 + c)),
                  pl.BlockSpec((None, s, PAIR), lambda bi, c: (bi, 0, kv0 + B_KV_HEADS + c)),
                  pl.BlockSpec((None,) + bias.shape[1:], lambda bi, c: (c, 0, 0, 0)),
                  pl.BlockSpec(memory_space=pltpu.SMEM)],
        out_specs=pl.BlockSpec((None, s, qw), lambda bi, c: (bi, 0, c)),
        out_shape=jax.ShapeDtypeStruct((b, s, B_Q_WIDTH), BF16),
        compiler_params=_params(2),
        name="attn_win",
    )(qkv_v, qkv_v, qkv_v, bias, sink)
    return o.reshape(b * s, B_Q_WIDTH)


def _post_kernel(x_ref, a_ref, bo_ref, g_ref, wga_ref, wgb_ref, bg_ref, wa_ref, wb_ref, wo_ref,
                 out_ref, mix_ref):
    x = x_ref[...]
    h = _rms(x, g_ref[...]).astype(BF16)
    a = a_ref[...]
    bo = bo_ref[...]
    cw = 256
    for c0 in range(0, D_MODEL, cw):
        cs = slice(c0, c0 + cw)
        ap = jnp.dot(a, wa_ref[:, cs], preferred_element_type=F32)
        bp = jnp.dot(bo, wb_ref[:, cs], preferred_element_type=F32)
        ga = jnp.dot(h, wga_ref[:, cs], preferred_element_type=F32) + bg_ref[:, cs]
        gb = jnp.dot(h, wgb_ref[:, cs], preferred_element_type=F32) + bg_ref[:, D_MODEL + c0:D_MODEL + c0 + cw]
        mix_ref[:, cs] = (jax.nn.sigmoid(ga) * ap + jax.nn.sigmoid(gb) * bp).astype(BF16)
    out_ref[...] = x + jnp.dot(mix_ref[...], wo_ref[...], preferred_element_type=F32)


def _post_call(x, a, bo, g, w_in, bg, wa, wb, wo):
    n = x.shape[0]
    t = ROW_TILE
    rows = lambda w: pl.BlockSpec((t, w), lambda i: (i, 0))
    gate_blk = GATE_OFF // D_MODEL
    return pl.pallas_call(
        _post_kernel,
        grid=(n // t,),
        in_specs=[rows(D_MODEL), rows(A_GROUP_WIDTH), rows(B_Q_WIDTH), _const_spec(g.shape),
                  _const_spec((D_MODEL, D_MODEL), (0, gate_blk)),
                  _const_spec((D_MODEL, D_MODEL), (0, gate_blk + 1)),
                  _const_spec(bg.shape), _const_spec(wa.shape), _const_spec(wb.shape), _const_spec(wo.shape)],
        out_specs=rows(D_MODEL),
        out_shape=jax.ShapeDtypeStruct((n, D_MODEL), F32),
        scratch_shapes=[pltpu.VMEM((t, D_MODEL), BF16)],
        compiler_params=_params(1),
        name="post_attn",
    )(x, a, bo, g, w_in, w_in, bg, wa, wb, wo)


def _gelu_tanh(x):
    k = -2.0 * LOG2E * math.sqrt(2.0 / math.pi)
    return x / (1.0 + jnp.exp2(x * (k + (k * 0.044715) * (x * x))))


def _mlp_kernel(xp_ref, x_ref, xn_ref, g_ref, wup_ref, cw_ref, cb_ref, wdn_ref, gf_ref, out_ref,
                act_ref, perm_ref, *, tiles_per_seq):
    t = x_ref.shape[0]
    nv = t // HALO
    ncb = D_MODEL // PAIR
    i = pl.program_id(0)
    keep_prev = jnp.where((i % tiles_per_seq) == 0, 0.0, 1.0)
    keep_next = jnp.where((i % tiles_per_seq) == tiles_per_seq - 1, 0.0, 1.0)
    x = x_ref[...]
    g = g_ref[...]
    h = _rms(x, g)
    for r in range(HALO):
        for c in range(ncb):
            perm_ref[c, pl.ds(r, nv, stride=HALO), :] = h[r * nv:(r + 1) * nv, c * PAIR:(c + 1) * PAIR]
    hp = jnp.concatenate([perm_ref[c] for c in range(ncb)], axis=1)
    he = jnp.concatenate([hp, _rms(xp_ref[...], g) * keep_prev, _rms(xn_ref[...], g) * keep_next],
                         axis=0).astype(BF16)

    sub = lax.broadcasted_iota(jnp.int32, (HALO, FF_CHUNK), 0)

    def conv(u, c0):
        w = cw_ref[:, c0:c0 + FF_CHUNK]
        main, halo_prev, halo_next = u[:t], u[t:t + HALO], u[t + HALO:]
        head = jnp.where(sub == 0, halo_prev[HALO - 1:], pltpu.roll(main[t - HALO:], 1, axis=0))
        tail = jnp.where(sub == HALO - 1, halo_next[:1], pltpu.roll(main[:HALO], HALO - 1, axis=0))
        prev = jnp.concatenate([head, main[:t - HALO]], axis=0)
        nxt = jnp.concatenate([main[HALO:], tail], axis=0)
        return prev * w[0:1] + main * w[1:2] + nxt * w[2:3] + cb_ref[:, c0:c0 + FF_CHUNK]

    for c0 in range(0, D_FF, FF_CHUNK):
        ug = jnp.dot(he, wup_ref[:, c0:c0 + FF_CHUNK], preferred_element_type=F32)
        uv = jnp.dot(he, wup_ref[:, D_FF + c0:D_FF + c0 + FF_CHUNK], preferred_element_type=F32)
        act_ref[:, c0:c0 + FF_CHUNK] = (_gelu_tanh(conv(ug, c0)) * conv(uv, D_FF + c0)).astype(BF16)
    yp = jnp.dot(act_ref[...], wdn_ref[...], preferred_element_type=F32)
    for c in range(ncb):
        perm_ref[c] = yp[:, c * PAIR:(c + 1) * PAIR]
    y = x + jnp.concatenate(
        [jnp.concatenate([perm_ref[c, pl.ds(r, nv, stride=HALO), :] for c in range(ncb)], axis=1)
         for r in range(HALO)], axis=0)
    out_ref[...] = _rms(y, gf_ref[...])


def _mlp_call(x, s, g, wup, cw, cb, wdn, gf):
    n = x.shape[0]
    t = ROW_TILE
    hb = t // HALO
    nhb = n // HALO
    consts = (g, wup, cw, cb, wdn, gf)
    return pl.pallas_call(
        functools.partial(_mlp_kernel, tiles_per_seq=s // t),
        grid=(n // t,),
        in_specs=[pl.BlockSpec((HALO, D_MODEL), lambda i: (jnp.maximum(i * hb - 1, 0), 0)),
                  pl.BlockSpec((t, D_MODEL), lambda i: (i, 0)),
                  pl.BlockSpec((HALO, D_MODEL), lambda i: (jnp.minimum((i + 1) * hb, nhb - 1), 0))]
        + [_const_spec(c.shape) for c in consts],
        out_specs=pl.BlockSpec((t, D_MODEL), lambda i: (i, 0)),
        out_shape=jax.ShapeDtypeStruct((n, D_MODEL), F32),
        scratch_shapes=[pltpu.VMEM((t, D_FF), BF16),
                        pltpu.VMEM((D_MODEL // PAIR, t, PAIR), F32)],
        compiler_params=_params(1),
        name="conv_mlp",
    )(x, x, x, *consts)


def _rel_bucket(rel):
    nb = N_BUCKETS // 2
    max_exact = nb // 2
    rel = np.asarray(rel, np.int32)
    ret = np.where(rel > 0, nb, 0)
    n = np.abs(rel)
    nf = np.maximum(n, 1).astype(np.float32)
    large = max_exact + (np.log(nf / np.float32(max_exact)) / np.float32(math.log(MAX_DISTANCE / max_exact))
                         * np.float32(nb - max_exact)).astype(np.int32)
    large = np.minimum(large, nb - 1)
    return (ret + np.where(n < max_exact, n, large)).astype(np.int32)


def _bias_tiles(rel_bias, head0, n_heads, dilation, win, side, shifts):
    offs = np.arange(-side, side + 1) * dilation
    line = rel_bias[_rel_bucket(offs)][:, head0:head0 + n_heads].T.astype(F32) * LOG2E
    k0 = QBLK + max(shifts) - side
    p = win + k0 + side + 1
    padded = jnp.pad(line, ((0, 0), (k0, p - k0 - (2 * side + 1))), constant_values=NEG)
    skew = jnp.tile(padded, (1, QBLK + 1))[:, :QBLK * (p - 1)].reshape(n_heads, QBLK, p - 1)
    return jnp.stack([skew[:, :, k0 - sh + side:k0 - sh + side + win] for sh in shifts], 0)


def _prep(w_in, b_gate, rel_bias, sink, w_a_out, w_b_out, w_o, w_up, conv_w, conv_b, w_down,
          g_attn, g_ffn, g_final):
    row = lambda v: v.reshape(1, -1).astype(F32)
    p = dict(
        w_in=w_in.astype(BF16), b_gate=row(b_gate),
        w_a_out=w_a_out.astype(BF16), w_b_out=w_b_out.astype(BF16), w_o=w_o.astype(BF16),
        w_up=w_up.astype(BF16), conv_w=conv_w.astype(F32), conv_b=row(conv_b),
        w_down=w_down.astype(BF16), g_attn=row(g_attn), g_ffn=row(g_ffn), g_final=row(g_final),
        rel_bias=rel_bias,
    )
    bb = _bias_tiles(rel_bias, A_HEADS, B_Q_HEADS, 1, B_WIN, B_RADIUS, (0, B_RADIUS, 2 * B_RADIUS))
    p["bias_b"] = bb.reshape(3, B_KV_HEADS, B_GROUP * QBLK, B_WIN).transpose(1, 0, 2, 3)
    p["sink"] = sink.astype(F32) * LOG2E
    return p


def _group_biases(rel_bias, s):
    out = []
    for gi, d in enumerate(DILATIONS):
        m = s // d
        win = min(A_WIN, m)
        shifts = (0,) if m == QBLK else (0, A_SIDE, 2 * A_SIDE)
        t = _bias_tiles(rel_bias, gi * A_HEADS_PER_GROUP, A_HEADS_PER_GROUP, d, win, A_SIDE, shifts)
        out.append(t.reshape(len(shifts), 2, 2 * QBLK, win))
    return out


def _trunk(x, p):
    b, s, _ = x.shape
    xf = x.reshape(b * s, D_MODEL)
    g1, g2, g3, qb = _qkv_call(xf, p["g_attn"], p["w_in"], b, s)
    a = _attn_a_call((g1, g2, g3), _group_biases(p["rel_bias"], s), b, s)
    bo = _attn_b_call(qb, p["bias_b"], p["sink"], b, s)
    x1 = _post_call(xf, a, bo, p["g_attn"], p["w_in"], p["b_gate"], p["w_a_out"], p["w_b_out"], p["w_o"])
    y = _mlp_call(x1, s, p["g_ffn"], p["w_up"], p["conv_w"], p["conv_b"], p["w_down"], p["g_final"])
    return y.reshape(b, s, D_MODEL)


def kernel(x_prompt, x_sample, g_attn, w_in, b_gate, rel_bias, sink, w_a_out, w_b_out, w_o, g_ffn,
           w_up, conv_w, conv_b, w_down, g_final):
    assert w_in.shape[0] == 1, "one layer: the final RMSNorm is fused into its MLP kernel"
    p = _prep(w_in[0], b_gate[0], rel_bias, sink[0], w_a_out[0], w_b_out[0], w_o[0], w_up[0],
              conv_w[0], conv_b[0], w_down[0], g_attn[0], g_ffn[0], g_final)
    return (_trunk(x_prompt, p), _trunk(x_sample, p))
```

```python
import functools
import math

import numpy as np
import jax
import jax.numpy as jnp
from jax import lax
from jax.experimental import pallas as pl
from jax.experimental.pallas import tpu as pltpu

F32 = jnp.float32
BF16 = jnp.bfloat16

D_MODEL = 1024
HEAD_DIM = 64
DIL_PATTERNS = ((128, 1), (512, 4), (2048, 16))
DILATIONS = tuple(d for _, d in DIL_PATTERNS)
A_HEADS_PER_GROUP = 4
A_GROUP_WIDTH = A_HEADS_PER_GROUP * HEAD_DIM
N_GROUPS = len(DIL_PATTERNS)
A_HEADS = N_GROUPS * A_HEADS_PER_GROUP
A_WIDTH = A_HEADS * HEAD_DIM
A_SIDE = 64
B_Q_HEADS = 8
B_KV_HEADS = 2
B_GROUP = B_Q_HEADS // B_KV_HEADS
B_Q_WIDTH = B_Q_HEADS * HEAD_DIM
B_KV_WIDTH = B_KV_HEADS * HEAD_DIM
B_RADIUS = 128
N_BUCKETS = 32
MAX_DISTANCE = 1024
D_FF = 2816
EPS = 1e-6
NEG = -1e30
LOG2E = math.log2(math.e)
Q_SCALE = HEAD_DIM ** -0.5 * LOG2E

KA_OFF, VA_OFF = A_WIDTH, 2 * A_WIDTH
QB_OFF = 3 * A_WIDTH
KVB_OFF = QB_OFF + B_Q_WIDTH
GATE_OFF = KVB_OFF + 2 * B_KV_WIDTH
QKV_COLS = GATE_OFF

PAIR = 2 * HEAD_DIM
PAIR_COLS = 3 * PAIR
GRP_COLS = 2 * PAIR_COLS
B_COLS = B_Q_WIDTH + 4 * PAIR
QBLK = 128
A_WIN = QBLK + 2 * A_SIDE
B_WIN = QBLK + 2 * B_RADIUS
ROW_TILE = 512
STEP_TILES = 2
MLP_STEP_TILES = 2
FF_CHUNK = 256
HALO = 8
VMEM_LIMIT = 56 * 1024 * 1024


def _params(n_axes):
    return pltpu.CompilerParams(dimension_semantics=("arbitrary",) * n_axes,
                                vmem_limit_bytes=VMEM_LIMIT)


def _const_spec(shape, index=None):
    index = (0,) * len(shape) if index is None else index
    return pl.BlockSpec(shape, lambda *_: index, pipeline_mode=pl.Buffered(1))


def _rms(x, g):
    ms = jnp.mean(x * x, axis=-1, keepdims=True)
    return x * lax.rsqrt(ms + EPS) * g


def _lane_lo(rows=QBLK):
    return lax.broadcasted_iota(jnp.int32, (rows, PAIR), 1) < HEAD_DIM


def _pair_rhs(v):
    lo = _lane_lo(v.shape[0])
    zero = jnp.zeros_like(v)
    ones_lo = jnp.where(lo, 1.0, 0.0).astype(v.dtype)
    ones_hi = jnp.where(lo, 0.0, 1.0).astype(v.dtype)
    top = jnp.concatenate([jnp.where(lo, v, zero), ones_lo], axis=1)
    bot = jnp.concatenate([jnp.where(lo, zero, v), ones_hi], axis=1)
    return jnp.concatenate([top, bot], axis=0)


def _qkv_kernel(x_ref, g_ref, w_ref, o1_ref, o2_ref, o3_ref, ob_ref, tmp_ref):
    for sub in range(x_ref.shape[0] // ROW_TILE):
        _qkv_rows(sub, x_ref, g_ref, w_ref, (o1_ref, o2_ref, o3_ref), ob_ref, tmp_ref.at[sub])


def _qkv_rows(sub, x_ref, g_ref, w_ref, o_refs, ob_ref, tmp_ref):
    t = ROW_TILE
    rows = slice(sub * t, (sub + 1) * t)
    h = _rms(x_ref[rows, :], g_ref[...]).astype(BF16)

    def proj(c0, n):
        return jnp.dot(h, w_ref[:, c0:c0 + n], preferred_element_type=F32)

    for gi, (o_ref, d) in enumerate(zip(o_refs, DILATIONS)):
        c0 = gi * A_GROUP_WIDTH
        parts = (proj(c0, A_GROUP_WIDTH) * Q_SCALE, proj(KA_OFF + c0, A_GROUP_WIDTH),
                 proj(VA_OFF + c0, A_GROUP_WIDTH))
        for p in range(2):
            for j, part in enumerate(parts):
                c = p * 3 + j
                val = part[:, p * PAIR:(p + 1) * PAIR]
                if d == 1:
                    o_ref[0, rows, c * PAIR:(c + 1) * PAIR] = val.astype(BF16)
                else:
                    tmp_ref[c] = val
        if d > 1:
            td = t // d
            for r in range(d):
                for c in range(GRP_COLS // PAIR):
                    o_ref[r, sub * td:(sub + 1) * td, c * PAIR:(c + 1) * PAIR] = (
                        tmp_ref[c, pl.ds(r, td, stride=d), :].astype(BF16))

    ob_ref[rows, :B_Q_WIDTH] = (proj(QB_OFF, B_Q_WIDTH) * Q_SCALE).astype(BF16)
    kv = proj(KVB_OFF, 2 * B_KV_WIDTH)
    lo = _lane_lo(t)
    for j in range(2):
        one = kv[:, j * PAIR:(j + 1) * PAIR]
        swapped = pltpu.roll(one, HEAD_DIM, axis=1)
        c0 = B_Q_WIDTH + j * 2 * PAIR
        ob_ref[rows, c0:c0 + PAIR] = jnp.where(lo, one, swapped).astype(BF16)
        ob_ref[rows, c0 + PAIR:c0 + 2 * PAIR] = jnp.where(lo, swapped, one).astype(BF16)


def _qkv_call(x, g, w_in, b, s):
    n = x.shape[0]
    t = STEP_TILES * ROW_TILE
    tps = s // t
    grp_shape = lambda d: jax.ShapeDtypeStruct((b, d, s // d, GRP_COLS), BF16)
    grp_spec = lambda d: pl.BlockSpec((None, d, t // d, GRP_COLS), lambda i: (i // tps, 0, i % tps, 0))
    return pl.pallas_call(
        _qkv_kernel,
        grid=(n // t,),
        in_specs=[pl.BlockSpec((t, D_MODEL), lambda i: (i, 0)), _const_spec(g.shape),
                  _const_spec((D_MODEL, QKV_COLS))],
        out_specs=[grp_spec(d) for d in DILATIONS] + [pl.BlockSpec((t, B_COLS), lambda i: (i, 0))],
        out_shape=[grp_shape(d) for d in DILATIONS] + [jax.ShapeDtypeStruct((n, B_COLS), BF16)],
        scratch_shapes=[pltpu.VMEM((STEP_TILES, GRP_COLS // PAIR, ROW_TILE, PAIR), F32)],
        compiler_params=_params(1),
        name="qkv_proj",
    )(x, g, w_in)


def _window(i, nblk, m, win, side):
    if nblk == 1:
        return 0, 0
    start = pl.multiple_of(jnp.clip(i * QBLK - side, 0, m - win), 64)
    var = jnp.where(i == 0, 0, jnp.where(i == nblk - 1, 2, 1))
    return start, var


def _attn_a_kernel(g1_ref, g2_ref, g3_ref, b1_ref, b2_ref, b3_ref, out_ref, acc_ref, m_ref, l_ref, *, s):
    lo = _lane_lo()
    hi = jnp.logical_not(lo)
    zero = jnp.zeros((QBLK, PAIR), BF16)

    def attend(q, k, v, bias):
        qs = jnp.concatenate([jnp.where(lo, q, zero), jnp.where(hi, q, zero)], axis=0)
        sc = lax.dot_general(qs, k, (((1,), (1,)), ((), ())), preferred_element_type=F32) + bias
        mx = jnp.max(sc, axis=-1, keepdims=True)
        e = jnp.exp2(sc - mx).astype(BF16)
        r = jnp.dot(jnp.concatenate([e[:QBLK], e[QBLK:]], axis=1), _pair_rhs(v), preferred_element_type=F32)
        return r[:, :PAIR], jnp.where(lo, mx[:QBLK], mx[QBLK:]), r[:, PAIR:]

    for gi, (ref, bias_ref, d) in enumerate(zip((g1_ref, g2_ref, g3_ref), (b1_ref, b2_ref, b3_ref),
                                                DILATIONS)):
        m = s // d
        nblk = m // QBLK
        win = min(A_WIN, m)
        shift = nblk.bit_length() - 1
        assert nblk == 1 << shift

        def body(j, carry, ref=ref, bias_ref=bias_ref, d=d, m=m, nblk=nblk, win=win, shift=shift, gi=gi):
            r, i = (0, j) if d == 1 else (j >> shift, j & (nblk - 1))
            q0 = pl.multiple_of(i * QBLK, QBLK)
            start, var = _window(i, nblk, m, win, A_SIDE)
            o, mm, ll = attend(ref[r, pl.ds(q0, QBLK), :PAIR], ref[r, pl.ds(start, win), PAIR:2 * PAIR],
                               ref[r, pl.ds(start, win), 2 * PAIR:], bias_ref[var])
            rows = pl.ds(q0, QBLK) if d == 1 else pl.ds(r + d * q0, QBLK, stride=d)
            if gi == 0:
                acc_ref[rows, :] = o
                m_ref[rows, :] = mm
                l_ref[rows, :] = ll
            else:
                m_old = m_ref[rows, :]
                m_new = jnp.maximum(m_old, mm)
                alpha = jnp.exp2(m_old - m_new)
                beta = jnp.exp2(mm - m_new)
                m_ref[rows, :] = m_new
                l_ref[rows, :] = alpha * l_ref[rows, :] + beta * ll
                acc_ref[rows, :] = alpha * acc_ref[rows, :] + beta * o
            return carry

        lax.fori_loop(0, d * nblk, body, 0, unroll=8)

    out_ref[...] = (acc_ref[...] / l_ref[...]).astype(BF16)


def _attn_a_call(groups, biases, b, s):
    in_specs = [pl.BlockSpec((None, d, s // d, PAIR_COLS), lambda bi, p: (bi, 0, 0, p)) for d in DILATIONS]
    in_specs += [pl.BlockSpec((bias.shape[0], None) + bias.shape[2:], lambda bi, p: (0, p, 0, 0))
                 for bias in biases]
    out = pl.pallas_call(
        functools.partial(_attn_a_kernel, s=s),
        grid=(b, 2),
        in_specs=in_specs,
        out_specs=pl.BlockSpec((None, s, PAIR), lambda bi, p: (bi, 0, p)),
        out_shape=jax.ShapeDtypeStruct((b, s, A_GROUP_WIDTH), BF16),
        scratch_shapes=[pltpu.VMEM((s, PAIR), F32)] * 3,
        compiler_params=_params(2),
        name="attn_dil",
    )(*groups, *biases)
    return out.reshape(b * s, A_GROUP_WIDTH)


def _attn_b_kernel(q_ref, k_ref, v_ref, bias_ref, sink_ref, o_ref, *, s):
    nblk = s // QBLK
    lo = _lane_lo()
    hi = jnp.logical_not(lo)
    zero = jnp.zeros((QBLK, PAIR), BF16)
    c = pl.program_id(1)

    def block(i, carry):
        q0 = pl.multiple_of(i * QBLK, QBLK)
        start, var = _window(i, nblk, s, B_WIN, B_RADIUS)
        kw = k_ref[pl.ds(start, B_WIN), :]
        rhs = _pair_rhs(v_ref[pl.ds(start, B_WIN), :])
        qs = []
        for p in range(2):
            q = q_ref[pl.ds(q0, QBLK), p * PAIR:(p + 1) * PAIR]
            qs += [jnp.where(lo, q, zero), jnp.where(hi, q, zero)]
        sc = lax.dot_general(jnp.concatenate(qs, axis=0), kw, (((1,), (1,)), ((), ())),
                             preferred_element_type=F32) + bias_ref[var]
        es, sinks = [], []
        for p in range(2):
            sk_e = sink_ref[c * B_GROUP + 2 * p]
            sk_o = sink_ref[c * B_GROUP + 2 * p + 1]
            sp = sc[p * 2 * QBLK:(p + 1) * 2 * QBLK]
            mx = jnp.max(sp, axis=-1, keepdims=True)
            mx = jnp.concatenate([jnp.maximum(mx[:QBLK], sk_e), jnp.maximum(mx[QBLK:], sk_o)], axis=0)
            e = jnp.exp2(sp - mx).astype(BF16)
            es.append(jnp.concatenate([e[:QBLK], e[QBLK:]], axis=1))
            sinks.append(jnp.where(lo, jnp.exp2(sk_e - mx[:QBLK]), jnp.exp2(sk_o - mx[QBLK:])))
        r = jnp.dot(jnp.concatenate(es, axis=0), rhs, preferred_element_type=F32)
        for p in range(2):
            rp = r[p * QBLK:(p + 1) * QBLK]
            o_ref[pl.ds(q0, QBLK), p * PAIR:(p + 1) * PAIR] = (rp[:, :PAIR] / (rp[:, PAIR:] + sinks[p])).astype(BF16)
        return carry

    lax.fori_loop(0, nblk, block, 0, unroll=4)


def _attn_b_call(qkv_b, bias, sink, b, s):
    qkv_v = qkv_b.reshape(b, s, B_COLS)
    qw = B_GROUP * HEAD_DIM
    kv0 = B_Q_WIDTH // PAIR
    o = pl.pallas_call(
        functools.partial(_attn_b_kernel, s=s),
        grid=(b, B_KV_HEADS),
        in_specs=[pl.BlockSpec((None, s, qw), lambda bi, c: (bi, 0, c)),
                  pl.BlockSpec((None, s, PAIR), lambda bi, c: (bi, 0, kv0 + c)),
                  pl.BlockSpec((None, s, PAIR), lambda bi, c: (bi, 0, kv0 + B_KV_HEADS + c)),
                  pl.BlockSpec((None,) + bias.shape[1:], lambda bi, c: (c, 0, 0, 0)),
                  pl.BlockSpec(memory_space=pltpu.SMEM)],
        out_specs=pl.BlockSpec((None, s, qw), lambda bi, c: (bi, 0, c)),
        out_shape=jax.ShapeDtypeStruct((b, s, B_Q_WIDTH), BF16),
        compiler_params=_params(2),
        name="attn_win",
    )(qkv_v, qkv_v, qkv_v, bias, sink)
    return o.reshape(b * s, B_Q_WIDTH)


def _post_kernel(x_ref, a_ref, bo_ref, g_ref, wga_ref, wgb_ref, bg_ref, wa_ref, wb_ref, wo_ref,
                 out_ref, mix_ref):
    cw = 256
    for sub in range(x_ref.shape[0] // ROW_TILE):
        rows = slice(sub * ROW_TILE, (sub + 1) * ROW_TILE)
        x = x_ref[rows, :]
        h = _rms(x, g_ref[...]).astype(BF16)
        a = a_ref[rows, :]
        bo = bo_ref[rows, :]
        for c0 in range(0, D_MODEL, cw):
            cs = slice(c0, c0 + cw)
            ap = jnp.dot(a, wa_ref[:, cs], preferred_element_type=F32)
            bp = jnp.dot(bo, wb_ref[:, cs], preferred_element_type=F32)
            ga = jnp.dot(h, wga_ref[:, cs], preferred_element_type=F32) + bg_ref[:, cs]
            gb = (jnp.dot(h, wgb_ref[:, cs], preferred_element_type=F32)
                  + bg_ref[:, D_MODEL + c0:D_MODEL + c0 + cw])
            mix_ref[rows, cs] = (jax.nn.sigmoid(ga) * ap + jax.nn.sigmoid(gb) * bp).astype(BF16)
        out_ref[rows, :] = x + jnp.dot(mix_ref[rows, :], wo_ref[...], preferred_element_type=F32)


def _post_call(x, a, bo, g, w_in, bg, wa, wb, wo):
    n = x.shape[0]
    t = STEP_TILES * ROW_TILE
    rows = lambda w: pl.BlockSpec((t, w), lambda i: (i, 0))
    gate_blk = GATE_OFF // D_MODEL
    return pl.pallas_call(
        _post_kernel,
        grid=(n // t,),
        in_specs=[rows(D_MODEL), rows(A_GROUP_WIDTH), rows(B_Q_WIDTH), _const_spec(g.shape),
                  _const_spec((D_MODEL, D_MODEL), (0, gate_blk)),
                  _const_spec((D_MODEL, D_MODEL), (0, gate_blk + 1)),
                  _const_spec(bg.shape), _const_spec(wa.shape), _const_spec(wb.shape), _const_spec(wo.shape)],
        out_specs=rows(D_MODEL),
        out_shape=jax.ShapeDtypeStruct((n, D_MODEL), F32),
        scratch_shapes=[pltpu.VMEM((t, D_MODEL), BF16)],
        compiler_params=_params(1),
        name="post_attn",
    )(x, a, bo, g, w_in, w_in, bg, wa, wb, wo)


def _gelu_tanh(x):
    k = -2.0 * LOG2E * math.sqrt(2.0 / math.pi)
    return x / (1.0 + jnp.exp2(x * (k + (k * 0.044715) * (x * x))))


def _mlp_kernel(xp_ref, x_ref, xn_ref, g_ref, wup_ref, cw_ref, cb_ref, wdn_ref, gf_ref, out_ref,
                act_ref, perm_ref, *, steps_per_seq):
    n_sub = x_ref.shape[0] // ROW_TILE
    i = pl.program_id(0)
    keep_prev = jnp.where((i % steps_per_seq) == 0, 0.0, 1.0)
    keep_next = jnp.where((i % steps_per_seq) == steps_per_seq - 1, 0.0, 1.0)
    g = g_ref[...]
    for sub in range(n_sub):
        r0 = sub * ROW_TILE
        halo_prev = (_rms(xp_ref[...], g) * keep_prev if sub == 0
                     else _rms(x_ref[r0 - HALO:r0, :], g))
        halo_next = (_rms(xn_ref[...], g) * keep_next if sub == n_sub - 1
                     else _rms(x_ref[r0 + ROW_TILE:r0 + ROW_TILE + HALO, :], g))
        _mlp_rows(x_ref.at[r0:r0 + ROW_TILE], halo_prev, halo_next, g, wup_ref, cw_ref, cb_ref, wdn_ref,
                  gf_ref, out_ref.at[r0:r0 + ROW_TILE], act_ref.at[sub], perm_ref.at[sub])


def _mlp_rows(x_ref, halo_prev, halo_next, g, wup_ref, cw_ref, cb_ref, wdn_ref, gf_ref, out_ref,
              act_ref, perm_ref):
    t = ROW_TILE
    nv = t // HALO
    ncb = D_MODEL // PAIR
    x = x_ref[...]
    h = _rms(x, g)
    for r in range(HALO):
        for c in range(ncb):
            perm_ref[c, pl.ds(r, nv, stride=HALO), :] = h[r * nv:(r + 1) * nv, c * PAIR:(c + 1) * PAIR]
    hp = jnp.concatenate([perm_ref[c] for c in range(ncb)], axis=1)
    he = jnp.concatenate([hp, halo_prev, halo_next], axis=0).astype(BF16)

    sub = lax.broadcasted_iota(jnp.int32, (HALO, FF_CHUNK), 0)

    def conv(u, c0):
        w = cw_ref[:, c0:c0 + FF_CHUNK]
        main, halo_prev, halo_next = u[:t], u[t:t + HALO], u[t + HALO:]
        head = jnp.where(sub == 0, halo_prev[HALO - 1:], pltpu.roll(main[t - HALO:], 1, axis=0))
        tail = jnp.where(sub == HALO - 1, halo_next[:1], pltpu.roll(main[:HALO], HALO - 1, axis=0))
        prev = jnp.concatenate([head, main[:t - HALO]], axis=0)
        nxt = jnp.concatenate([main[HALO:], tail], axis=0)
        return prev * w[0:1] + main * w[1:2] + nxt * w[2:3] + cb_ref[:, c0:c0 + FF_CHUNK]

    for c0 in range(0, D_FF, FF_CHUNK):
        ug = jnp.dot(he, wup_ref[:, c0:c0 + FF_CHUNK], preferred_element_type=F32)
        uv = jnp.dot(he, wup_ref[:, D_FF + c0:D_FF + c0 + FF_CHUNK], preferred_element_type=F32)
        act_ref[:, c0:c0 + FF_CHUNK] = (_gelu_tanh(conv(ug, c0)) * conv(uv, D_FF + c0)).astype(BF16)
    yp = jnp.dot(act_ref[...], wdn_ref[...], preferred_element_type=F32)
    for c in range(ncb):
        perm_ref[c] = yp[:, c * PAIR:(c + 1) * PAIR]
    y = x + jnp.concatenate(
        [jnp.concatenate([perm_ref[c, pl.ds(r, nv, stride=HALO), :] for c in range(ncb)], axis=1)
         for r in range(HALO)], axis=0)
    out_ref[...] = _rms(y, gf_ref[...])


def _mlp_call(x, s, g, wup, cw, cb, wdn, gf):
    n = x.shape[0]
    t = MLP_STEP_TILES * ROW_TILE
    hb = t // HALO
    nhb = n // HALO
    consts = (g, wup, cw, cb, wdn, gf)
    return pl.pallas_call(
        functools.partial(_mlp_kernel, steps_per_seq=s // t),
        grid=(n // t,),
        in_specs=[pl.BlockSpec((HALO, D_MODEL), lambda i: (jnp.maximum(i * hb - 1, 0), 0)),
                  pl.BlockSpec((t, D_MODEL), lambda i: (i, 0)),
                  pl.BlockSpec((HALO, D_MODEL), lambda i: (jnp.minimum((i + 1) * hb, nhb - 1), 0))]
        + [_const_spec(c.shape) for c in consts],
        out_specs=pl.BlockSpec((t, D_MODEL), lambda i: (i, 0)),
        out_shape=jax.ShapeDtypeStruct((n, D_MODEL), F32),
        scratch_shapes=[pltpu.VMEM((MLP_STEP_TILES, ROW_TILE, D_FF), BF16),
                        pltpu.VMEM((MLP_STEP_TILES, D_MODEL // PAIR, ROW_TILE, PAIR), F32)],
        compiler_params=_params(1),
        name="conv_mlp",
    )(x, x, x, *consts)


def _rel_bucket(rel):
    nb = N_BUCKETS // 2
    max_exact = nb // 2
    rel = np.asarray(rel, np.int32)
    ret = np.where(rel > 0, nb, 0)
    n = np.abs(rel)
    nf = np.maximum(n, 1).astype(np.float32)
    large = max_exact + (np.log(nf / np.float32(max_exact)) / np.float32(math.log(MAX_DISTANCE / max_exact))
                         * np.float32(nb - max_exact)).astype(np.int32)
    large = np.minimum(large, nb - 1)
    return (ret + np.where(n < max_exact, n, large)).astype(np.int32)


def _bias_tiles(rel_bias, head0, n_heads, dilation, win, side, shifts):
    offs = np.arange(-side, side + 1) * dilation
    line = rel_bias[_rel_bucket(offs)][:, head0:head0 + n_heads].T.astype(F32) * LOG2E
    k0 = QBLK + max(shifts) - side
    p = win + k0 + side + 1
    padded = jnp.pad(line, ((0, 0), (k0, p - k0 - (2 * side + 1))), constant_values=NEG)
    skew = jnp.tile(padded, (1, QBLK + 1))[:, :QBLK * (p - 1)].reshape(n_heads, QBLK, p - 1)
    return jnp.stack([skew[:, :, k0 - sh + side:k0 - sh + side + win] for sh in shifts], 0)


def _prep(w_in, b_gate, rel_bias, sink, w_a_out, w_b_out, w_o, w_up, conv_w, conv_b, w_down,
          g_attn, g_ffn, g_final):
    row = lambda v: v.reshape(1, -1).astype(F32)
    p = dict(
        w_in=w_in.astype(BF16), b_gate=row(b_gate),
        w_a_out=w_a_out.astype(BF16), w_b_out=w_b_out.astype(BF16), w_o=w_o.astype(BF16),
        w_up=w_up.astype(BF16), conv_w=conv_w.astype(F32), conv_b=row(conv_b),
        w_down=w_down.astype(BF16), g_attn=row(g_attn), g_ffn=row(g_ffn), g_final=row(g_final),
        rel_bias=rel_bias,
    )
    bb = _bias_tiles(rel_bias, A_HEADS, B_Q_HEADS, 1, B_WIN, B_RADIUS, (0, B_RADIUS, 2 * B_RADIUS))
    p["bias_b"] = bb.reshape(3, B_KV_HEADS, B_GROUP * QBLK, B_WIN).transpose(1, 0, 2, 3)
    p["sink"] = sink.astype(F32) * LOG2E
    return p


def _group_biases(rel_bias, s):
    out = []
    for gi, d in enumerate(DILATIONS):
        m = s // d
        win = min(A_WIN, m)
        shifts = (0,) if m == QBLK else (0, A_SIDE, 2 * A_SIDE)
        t = _bias_tiles(rel_bias, gi * A_HEADS_PER_GROUP, A_HEADS_PER_GROUP, d, win, A_SIDE, shifts)
        out.append(t.reshape(len(shifts), 2, 2 * QBLK, win))
    return out


def _trunk(x, p):
    b, s, _ = x.shape
    xf = x.reshape(b * s, D_MODEL)
    g1, g2, g3, qb = _qkv_call(xf, p["g_attn"], p["w_in"], b, s)
    a = _attn_a_call((g1, g2, g3), _group_biases(p["rel_bias"], s), b, s)
    bo = _attn_b_call(qb, p["bias_b"], p["sink"], b, s)
    x1 = _post_call(xf, a, bo, p["g_attn"], p["w_in"], p["b_gate"], p["w_a_out"], p["w_b_out"], p["w_o"])
    y = _mlp_call(x1, s, p["g_ffn"], p["w_up"], p["conv_w"], p["conv_b"], p["w_down"], p["g_final"])
    return y.reshape(b, s, D_MODEL)


def kernel(x_prompt, x_sample, g_attn, w_in, b_gate, rel_bias, sink, w_a_out, w_b_out, w_o, g_ffn,
           w_up, conv_w, conv_b, w_down, g_final):
    assert w_in.shape[0] == 1, "one layer: the final RMSNorm is fused into its MLP kernel"
    p = _prep(w_in[0], b_gate[0], rel_bias, sink[0], w_a_out[0], w_b_out[0], w_o[0], w_up[0],
              conv_w[0], conv_b[0], w_down[0], g_attn[0], g_ffn[0], g_final)
    return (_trunk(x_prompt, p), _trunk(x_sample, p))
```

```python
import functools
import math

import numpy as np
import jax
import jax.numpy as jnp
from jax import lax
from jax.experimental import pallas as pl
from jax.experimental.pallas import tpu as pltpu

F32 = jnp.float32
BF16 = jnp.bfloat16

D_MODEL = 1024
HEAD_DIM = 64
DIL_PATTERNS = ((128, 1), (512, 4), (2048, 16))
DILATIONS = tuple(d for _, d in DIL_PATTERNS)
A_HEADS_PER_GROUP = 4
A_GROUP_WIDTH = A_HEADS_PER_GROUP * HEAD_DIM
N_GROUPS = len(DIL_PATTERNS)
A_HEADS = N_GROUPS * A_HEADS_PER_GROUP
A_WIDTH = A_HEADS * HEAD_DIM
A_SIDE = 64
B_Q_HEADS = 8
B_KV_HEADS = 2
B_GROUP = B_Q_HEADS // B_KV_HEADS
B_Q_WIDTH = B_Q_HEADS * HEAD_DIM
B_KV_WIDTH = B_KV_HEADS * HEAD_DIM
B_RADIUS = 128
N_BUCKETS = 32
MAX_DISTANCE = 1024
D_FF = 2816
EPS = 1e-6
NEG = -1e30
LOG2E = math.log2(math.e)
Q_SCALE = HEAD_DIM ** -0.5 * LOG2E

KA_OFF, VA_OFF = A_WIDTH, 2 * A_WIDTH
QB_OFF = 3 * A_WIDTH
KVB_OFF = QB_OFF + B_Q_WIDTH
GATE_OFF = KVB_OFF + 2 * B_KV_WIDTH
QKV_COLS = GATE_OFF

PAIR = 2 * HEAD_DIM
PAIR_COLS = 3 * PAIR
GRP_COLS = 2 * PAIR_COLS
B_COLS = B_Q_WIDTH + 4 * PAIR
QBLK = 128
A_WIN = QBLK + 2 * A_SIDE
B_WIN = QBLK + 2 * B_RADIUS
ROW_TILE = 512
STEP_TILES = 2
MLP_STEP_TILES = 2
FF_CHUNK = 256
HALO = 8
VMEM_LIMIT = 56 * 1024 * 1024


def _params(n_axes):
    return pltpu.CompilerParams(dimension_semantics=("arbitrary",) * n_axes,
                                vmem_limit_bytes=VMEM_LIMIT)


def _const_spec(shape, index=None):
    index = (0,) * len(shape) if index is None else index
    return pl.BlockSpec(shape, lambda *_: index, pipeline_mode=pl.Buffered(1))


def _rms(x, g):
    ms = jnp.mean(x * x, axis=-1, keepdims=True)
    return x * lax.rsqrt(ms + EPS) * g


def _lane_lo(rows=QBLK):
    return lax.broadcasted_iota(jnp.int32, (rows, PAIR), 1) < HEAD_DIM


def _pair_rhs(v):
    lo = _lane_lo(v.shape[0])
    zero = jnp.zeros_like(v)
    ones_lo = jnp.where(lo, 1.0, 0.0).astype(v.dtype)
    ones_hi = jnp.where(lo, 0.0, 1.0).astype(v.dtype)
    top = jnp.concatenate([jnp.where(lo, v, zero), ones_lo], axis=1)
    bot = jnp.concatenate([jnp.where(lo, zero, v), ones_hi], axis=1)
    return jnp.concatenate([top, bot], axis=0)


def _pair_scores(q, k):
    lo = _lane_lo(k.shape[0])
    zero = jnp.zeros_like(k)
    kbd = jnp.concatenate([jnp.where(lo, k, zero), jnp.where(lo, zero, k)], axis=0)
    return lax.dot_general(q, kbd, (((1,), (1,)), ((), ())), preferred_element_type=F32)


def _pair_softmax(sc, floors=(None, None)):
    win = sc.shape[1] // 2
    out = []
    for half, floor in enumerate(floors):
        s = sc[:, half * win:(half + 1) * win]
        m = jnp.broadcast_to(jnp.max(s, axis=-1, keepdims=True), (sc.shape[0], PAIR))
        if floor is not None:
            m = jnp.maximum(m, floor)
        out.append((m, jnp.exp2(s - jnp.concatenate([m] * (win // PAIR), axis=1))))
    return out


def _qkv_kernel(x_ref, g_ref, w_ref, o1_ref, o2_ref, o3_ref, ob_ref, tmp_ref):
    for sub in range(x_ref.shape[0] // ROW_TILE):
        _qkv_rows(sub, x_ref, g_ref, w_ref, (o1_ref, o2_ref, o3_ref), ob_ref, tmp_ref.at[sub])


def _qkv_rows(sub, x_ref, g_ref, w_ref, o_refs, ob_ref, tmp_ref):
    t = ROW_TILE
    rows = slice(sub * t, (sub + 1) * t)
    h = _rms(x_ref[rows, :], g_ref[...]).astype(BF16)

    def proj(c0, n):
        return jnp.dot(h, w_ref[:, c0:c0 + n], preferred_element_type=F32)

    for gi, (o_ref, d) in enumerate(zip(o_refs, DILATIONS)):
        c0 = gi * A_GROUP_WIDTH
        parts = (proj(c0, A_GROUP_WIDTH) * Q_SCALE, proj(KA_OFF + c0, A_GROUP_WIDTH),
                 proj(VA_OFF + c0, A_GROUP_WIDTH))
        for p in range(2):
            for j, part in enumerate(parts):
                c = p * 3 + j
                val = part[:, p * PAIR:(p + 1) * PAIR]
                if d == 1:
                    o_ref[0, rows, c * PAIR:(c + 1) * PAIR] = val.astype(BF16)
                else:
                    tmp_ref[c] = val
        if d > 1:
            td = t // d
            for r in range(d):
                for c in range(GRP_COLS // PAIR):
                    o_ref[r, sub * td:(sub + 1) * td, c * PAIR:(c + 1) * PAIR] = (
                        tmp_ref[c, pl.ds(r, td, stride=d), :].astype(BF16))

    ob_ref[rows, :B_Q_WIDTH] = (proj(QB_OFF, B_Q_WIDTH) * Q_SCALE).astype(BF16)
    kv = proj(KVB_OFF, 2 * B_KV_WIDTH)
    lo = _lane_lo(t)
    for j in range(2):
        one = kv[:, j * PAIR:(j + 1) * PAIR]
        swapped = pltpu.roll(one, HEAD_DIM, axis=1)
        c0 = B_Q_WIDTH + j * 2 * PAIR
        ob_ref[rows, c0:c0 + PAIR] = jnp.where(lo, one, swapped).astype(BF16)
        ob_ref[rows, c0 + PAIR:c0 + 2 * PAIR] = jnp.where(lo, swapped, one).astype(BF16)


def _qkv_call(x, g, w_in, b, s):
    n = x.shape[0]
    t = STEP_TILES * ROW_TILE
    tps = s // t
    grp_shape = lambda d: jax.ShapeDtypeStruct((b, d, s // d, GRP_COLS), BF16)
    grp_spec = lambda d: pl.BlockSpec((None, d, t // d, GRP_COLS), lambda i: (i // tps, 0, i % tps, 0))
    return pl.pallas_call(
        _qkv_kernel,
        grid=(n // t,),
        in_specs=[pl.BlockSpec((t, D_MODEL), lambda i: (i, 0)), _const_spec(g.shape),
                  _const_spec((D_MODEL, QKV_COLS))],
        out_specs=[grp_spec(d) for d in DILATIONS] + [pl.BlockSpec((t, B_COLS), lambda i: (i, 0))],
        out_shape=[grp_shape(d) for d in DILATIONS] + [jax.ShapeDtypeStruct((n, B_COLS), BF16)],
        scratch_shapes=[pltpu.VMEM((STEP_TILES, GRP_COLS // PAIR, ROW_TILE, PAIR), F32)],
        compiler_params=_params(1),
        name="qkv_proj",
    )(x, g, w_in)


def _window(i, nblk, m, win, side):
    if nblk == 1:
        return 0, 0
    start = pl.multiple_of(jnp.clip(i * QBLK - side, 0, m - win), 64)
    var = jnp.where(i == 0, 0, jnp.where(i == nblk - 1, 2, 1))
    return start, var


def _attn_a_kernel(g1_ref, g2_ref, g3_ref, b1_ref, b2_ref, b3_ref, out_ref, acc_ref, m_ref, l_ref, *, s):
    lo = _lane_lo()

    def attend(q, k, v, bias):
        (m_e, e_e), (m_o, e_o) = _pair_softmax(_pair_scores(q, k) + bias)
        e = jnp.concatenate([e_e, e_o], axis=1).astype(BF16)
        r = jnp.dot(e, _pair_rhs(v), preferred_element_type=F32)
        return r[:, :PAIR], jnp.where(lo, m_e, m_o), r[:, PAIR:]

    for gi, (ref, bias_ref, d) in enumerate(zip((g1_ref, g2_ref, g3_ref), (b1_ref, b2_ref, b3_ref),
                                                DILATIONS)):
        m = s // d
        nblk = m // QBLK
        win = min(A_WIN, m)
        shift = nblk.bit_length() - 1
        assert nblk == 1 << shift

        def body(j, carry, ref=ref, bias_ref=bias_ref, d=d, m=m, nblk=nblk, win=win, shift=shift, gi=gi):
            r, i = (0, j) if d == 1 else (j >> shift, j & (nblk - 1))
            q0 = pl.multiple_of(i * QBLK, QBLK)
            start, var = _window(i, nblk, m, win, A_SIDE)
            o, mm, ll = attend(ref[r, pl.ds(q0, QBLK), :PAIR], ref[r, pl.ds(start, win), PAIR:2 * PAIR],
                               ref[r, pl.ds(start, win), 2 * PAIR:], bias_ref[var])
            rows = pl.ds(q0, QBLK) if d == 1 else pl.ds(r + d * q0, QBLK, stride=d)
            if gi == 0:
                acc_ref[rows, :] = o
                m_ref[rows, :] = mm
                l_ref[rows, :] = ll
            else:
                m_old = m_ref[rows, :]
                m_new = jnp.maximum(m_old, mm)
                alpha = jnp.exp2(m_old - m_new)
                beta = jnp.exp2(mm - m_new)
                m_ref[rows, :] = m_new
                l_ref[rows, :] = alpha * l_ref[rows, :] + beta * ll
                acc_ref[rows, :] = alpha * acc_ref[rows, :] + beta * o
            return carry

        lax.fori_loop(0, d * nblk, body, 0, unroll=8)

    out_ref[...] = (acc_ref[...] / l_ref[...]).astype(BF16)


def _attn_a_call(groups, biases, b, s):
    in_specs = [pl.BlockSpec((None, d, s // d, PAIR_COLS), lambda bi, p: (bi, 0, 0, p)) for d in DILATIONS]
    in_specs += [pl.BlockSpec((bias.shape[0], None) + bias.shape[2:], lambda bi, p: (0, p, 0, 0))
                 for bias in biases]
    out = pl.pallas_call(
        functools.partial(_attn_a_kernel, s=s),
        grid=(b, 2),
        in_specs=in_specs,
        out_specs=pl.BlockSpec((None, s, PAIR), lambda bi, p: (bi, 0, p)),
        out_shape=jax.ShapeDtypeStruct((b, s, A_GROUP_WIDTH), BF16),
        scratch_shapes=[pltpu.VMEM((s, PAIR), F32)] * 3,
        compiler_params=_params(2),
        name="attn_dil",
    )(*groups, *biases)
    return out.reshape(b * s, A_GROUP_WIDTH)


def _attn_b_kernel(q_ref, k_ref, v_ref, bias_ref, sink_ref, o_ref, *, s):
    nblk = s // QBLK
    lo = _lane_lo()
    c = pl.program_id(1)

    def block(i, carry):
        q0 = pl.multiple_of(i * QBLK, QBLK)
        start, var = _window(i, nblk, s, B_WIN, B_RADIUS)
        q = jnp.concatenate([q_ref[pl.ds(q0, QBLK), :PAIR], q_ref[pl.ds(q0, QBLK), PAIR:]], axis=0)
        sc = _pair_scores(q, k_ref[pl.ds(start, B_WIN), :])
        rhs = _pair_rhs(v_ref[pl.ds(start, B_WIN), :])
        es, sinks = [], []
        for p in range(2):
            sk_e = sink_ref[c * B_GROUP + 2 * p]
            sk_o = sink_ref[c * B_GROUP + 2 * p + 1]
            (m_e, e_e), (m_o, e_o) = _pair_softmax(sc[p * QBLK:(p + 1) * QBLK] + bias_ref[var, p], (sk_e, sk_o))
            es.append(jnp.concatenate([e_e, e_o], axis=1).astype(BF16))
            sinks.append(jnp.where(lo, jnp.exp2(sk_e - m_e), jnp.exp2(sk_o - m_o)))
        r = jnp.dot(jnp.concatenate(es, axis=0), rhs, preferred_element_type=F32)
        for p in range(2):
            rp = r[p * QBLK:(p + 1) * QBLK]
            o_ref[pl.ds(q0, QBLK), p * PAIR:(p + 1) * PAIR] = (rp[:, :PAIR] / (rp[:, PAIR:] + sinks[p])).astype(BF16)
        return carry

    lax.fori_loop(0, nblk, block, 0, unroll=8)


def _attn_b_call(qkv_b, bias, sink, b, s):
    qkv_v = qkv_b.reshape(b, s, B_COLS)
    qw = B_GROUP * HEAD_DIM
    kv0 = B_Q_WIDTH // PAIR
    o = pl.pallas_call(
        functools.partial(_attn_b_kernel, s=s),
        grid=(b, B_KV_HEADS),
        in_specs=[pl.BlockSpec((None, s, qw), lambda bi, c: (bi, 0, c)),
                  pl.BlockSpec((None, s, PAIR), lambda bi, c: (bi, 0, kv0 + c)),
                  pl.BlockSpec((None, s, PAIR), lambda bi, c: (bi, 0, kv0 + B_KV_HEADS + c)),
                  pl.BlockSpec((None,) + bias.shape[1:], lambda bi, c: (c, 0, 0, 0, 0)),
                  pl.BlockSpec(memory_space=pltpu.SMEM)],
        out_specs=pl.BlockSpec((None, s, qw), lambda bi, c: (bi, 0, c)),
        out_shape=jax.ShapeDtypeStruct((b, s, B_Q_WIDTH), BF16),
        compiler_params=_params(2),
        name="attn_win",
    )(qkv_v, qkv_v, qkv_v, bias, sink)
    return o.reshape(b * s, B_Q_WIDTH)


def _post_kernel(x_ref, a_ref, bo_ref, g_ref, wga_ref, wgb_ref, bg_ref, wa_ref, wb_ref, wo_ref,
                 out_ref, mix_ref):
    cw = 256
    for sub in range(x_ref.shape[0] // ROW_TILE):
        rows = slice(sub * ROW_TILE, (sub + 1) * ROW_TILE)
        x = x_ref[rows, :]
        h = _rms(x, g_ref[...]).astype(BF16)
        a = a_ref[rows, :]
        bo = bo_ref[rows, :]
        for c0 in range(0, D_MODEL, cw):
            cs = slice(c0, c0 + cw)
            ap = jnp.dot(a, wa_ref[:, cs], preferred_element_type=F32)
            bp = jnp.dot(bo, wb_ref[:, cs], preferred_element_type=F32)
            ga = jnp.dot(h, wga_ref[:, cs], preferred_element_type=F32) + bg_ref[:, cs]
            gb = (jnp.dot(h, wgb_ref[:, cs], preferred_element_type=F32)
                  + bg_ref[:, D_MODEL + c0:D_MODEL + c0 + cw])
            mix_ref[rows, cs] = (jax.nn.sigmoid(ga) * ap + jax.nn.sigmoid(gb) * bp).astype(BF16)
        out_ref[rows, :] = x + jnp.dot(mix_ref[rows, :], wo_ref[...], preferred_element_type=F32)


def _post_call(x, a, bo, g, w_in, bg, wa, wb, wo):
    n = x.shape[0]
    t = STEP_TILES * ROW_TILE
    rows = lambda w: pl.BlockSpec((t, w), lambda i: (i, 0))
    gate_blk = GATE_OFF // D_MODEL
    return pl.pallas_call(
        _post_kernel,
        grid=(n // t,),
        in_specs=[rows(D_MODEL), rows(A_GROUP_WIDTH), rows(B_Q_WIDTH), _const_spec(g.shape),
                  _const_spec((D_MODEL, D_MODEL), (0, gate_blk)),
                  _const_spec((D_MODEL, D_MODEL), (0, gate_blk + 1)),
                  _const_spec(bg.shape), _const_spec(wa.shape), _const_spec(wb.shape), _const_spec(wo.shape)],
        out_specs=rows(D_MODEL),
        out_shape=jax.ShapeDtypeStruct((n, D_MODEL), F32),
        scratch_shapes=[pltpu.VMEM((t, D_MODEL), BF16)],
        compiler_params=_params(1),
        name="post_attn",
    )(x, a, bo, g, w_in, w_in, bg, wa, wb, wo)


def _gelu_tanh(x):
    k = -2.0 * LOG2E * math.sqrt(2.0 / math.pi)
    return x / (1.0 + jnp.exp2(x * (k + (k * 0.044715) * (x * x))))


def _mlp_kernel(xp_ref, x_ref, xn_ref, g_ref, wup_ref, cw_ref, cb_ref, wdn_ref, gf_ref, out_ref,
                act_ref, perm_ref, *, steps_per_seq):
    n_sub = x_ref.shape[0] // ROW_TILE
    i = pl.program_id(0)
    keep_prev = jnp.where((i % steps_per_seq) == 0, 0.0, 1.0)
    keep_next = jnp.where((i % steps_per_seq) == steps_per_seq - 1, 0.0, 1.0)
    g = g_ref[...]
    for sub in range(n_sub):
        r0 = sub * ROW_TILE
        halo_prev = (_rms(xp_ref[...], g) * keep_prev if sub == 0
                     else _rms(x_ref[r0 - HALO:r0, :], g))
        halo_next = (_rms(xn_ref[...], g) * keep_next if sub == n_sub - 1
                     else _rms(x_ref[r0 + ROW_TILE:r0 + ROW_TILE + HALO, :], g))
        _mlp_rows(x_ref.at[r0:r0 + ROW_TILE], halo_prev, halo_next, g, wup_ref, cw_ref, cb_ref, wdn_ref,
                  gf_ref, out_ref.at[r0:r0 + ROW_TILE], act_ref.at[sub], perm_ref.at[sub])


def _mlp_rows(x_ref, halo_prev, halo_next, g, wup_ref, cw_ref, cb_ref, wdn_ref, gf_ref, out_ref,
              act_ref, perm_ref):
    t = ROW_TILE
    nv = t // HALO
    ncb = D_MODEL // PAIR
    x = x_ref[...]
    h = _rms(x, g)
    for r in range(HALO):
        for c in range(ncb):
            perm_ref[c, pl.ds(r, nv, stride=HALO), :] = h[r * nv:(r + 1) * nv, c * PAIR:(c + 1) * PAIR]
    hp = jnp.concatenate([perm_ref[c] for c in range(ncb)], axis=1)
    he = jnp.concatenate([hp, halo_prev, halo_next], axis=0).astype(BF16)

    def conv(u, c0):
        n = u.shape[1]
        w = cw_ref[:, c0:c0 + n]
        sub = lax.broadcasted_iota(jnp.int32, (HALO, n), 0)
        main, halo_prev, halo_next = u[:t], u[t:t + HALO], u[t + HALO:]
        head = jnp.where(sub == 0, halo_prev[HALO - 1:], pltpu.roll(main[t - HALO:], 1, axis=0))
        tail = jnp.where(sub == HALO - 1, halo_next[:1], pltpu.roll(main[:HALO], HALO - 1, axis=0))
        prev = jnp.concatenate([head, main[:t - HALO]], axis=0)
        nxt = jnp.concatenate([main[HALO:], tail], axis=0)
        return prev * w[0:1] + main * w[1:2] + nxt * w[2:3] + cb_ref[:, c0:c0 + n]

    for c0 in range(0, D_FF, FF_CHUNK):
        c1 = min(c0 + FF_CHUNK, D_FF)
        ug = jnp.dot(he, wup_ref[:, c0:c1], preferred_element_type=F32)
        uv = jnp.dot(he, wup_ref[:, D_FF + c0:D_FF + c1], preferred_element_type=F32)
        act_ref[:, c0:c1] = (_gelu_tanh(conv(ug, c0)) * conv(uv, D_FF + c0)).astype(BF16)
    yp = jnp.dot(act_ref[...], wdn_ref[...], preferred_element_type=F32)
    for c in range(ncb):
        perm_ref[c] = yp[:, c * PAIR:(c + 1) * PAIR]
    y = x + jnp.concatenate(
        [jnp.concatenate([perm_ref[c, pl.ds(r, nv, stride=HALO), :] for c in range(ncb)], axis=1)
         for r in range(HALO)], axis=0)
    out_ref[...] = _rms(y, gf_ref[...])


def _mlp_call(x, s, g, wup, cw, cb, wdn, gf):
    n = x.shape[0]
    t = MLP_STEP_TILES * ROW_TILE
    hb = t // HALO
    nhb = n // HALO
    consts = (g, wup, cw, cb, wdn, gf)
    return pl.pallas_call(
        functools.partial(_mlp_kernel, steps_per_seq=s // t),
        grid=(n // t,),
        in_specs=[pl.BlockSpec((HALO, D_MODEL), lambda i: (jnp.maximum(i * hb - 1, 0), 0)),
                  pl.BlockSpec((t, D_MODEL), lambda i: (i, 0)),
                  pl.BlockSpec((HALO, D_MODEL), lambda i: (jnp.minimum((i + 1) * hb, nhb - 1), 0))]
        + [_const_spec(c.shape) for c in consts],
        out_specs=pl.BlockSpec((t, D_MODEL), lambda i: (i, 0)),
        out_shape=jax.ShapeDtypeStruct((n, D_MODEL), F32),
        scratch_shapes=[pltpu.VMEM((MLP_STEP_TILES, ROW_TILE, D_FF), BF16),
                        pltpu.VMEM((MLP_STEP_TILES, D_MODEL // PAIR, ROW_TILE, PAIR), F32)],
        compiler_params=_params(1),
        name="conv_mlp",
    )(x, x, x, *consts)


def _rel_bucket(rel):
    nb = N_BUCKETS // 2
    max_exact = nb // 2
    rel = np.asarray(rel, np.int32)
    ret = np.where(rel > 0, nb, 0)
    n = np.abs(rel)
    nf = np.maximum(n, 1).astype(np.float32)
    large = max_exact + (np.log(nf / np.float32(max_exact)) / np.float32(math.log(MAX_DISTANCE / max_exact))
                         * np.float32(nb - max_exact)).astype(np.int32)
    large = np.minimum(large, nb - 1)
    return (ret + np.where(n < max_exact, n, large)).astype(np.int32)


def _bias_tiles(rel_bias, head0, n_heads, dilation, win, side, shifts):
    offs = np.arange(-side, side + 1) * dilation
    line = rel_bias[_rel_bucket(offs)][:, head0:head0 + n_heads].T.astype(F32) * LOG2E
    k0 = QBLK + max(shifts) - side
    p = win + k0 + side + 1
    padded = jnp.pad(line, ((0, 0), (k0, p - k0 - (2 * side + 1))), constant_values=NEG)
    skew = jnp.tile(padded, (1, QBLK + 1))[:, :QBLK * (p - 1)].reshape(n_heads, QBLK, p - 1)
    return jnp.stack([skew[:, :, k0 - sh + side:k0 - sh + side + win] for sh in shifts], 0)


def _prep(w_in, b_gate, rel_bias, sink, w_a_out, w_b_out, w_o, w_up, conv_w, conv_b, w_down,
          g_attn, g_ffn, g_final):
    row = lambda v: v.reshape(1, -1).astype(F32)
    p = dict(
        w_in=w_in.astype(BF16), b_gate=row(b_gate),
        w_a_out=w_a_out.astype(BF16), w_b_out=w_b_out.astype(BF16), w_o=w_o.astype(BF16),
        w_up=w_up.astype(BF16), conv_w=conv_w.astype(F32), conv_b=row(conv_b),
        w_down=w_down.astype(BF16), g_attn=row(g_attn), g_ffn=row(g_ffn), g_final=row(g_final),
        rel_bias=rel_bias,
    )
    bb = _bias_tiles(rel_bias, A_HEADS, B_Q_HEADS, 1, B_WIN, B_RADIUS, (0, B_RADIUS, 2 * B_RADIUS))
    p["bias_b"] = _pair_lanes(bb.reshape(3, B_KV_HEADS, 2, 2, QBLK, B_WIN)).transpose(1, 0, 2, 3, 4)
    p["sink"] = sink.astype(F32) * LOG2E
    return p


def _pair_lanes(t):
    return jnp.concatenate([t[..., 0, :, :], t[..., 1, :, :]], axis=-1)


def _group_biases(rel_bias, s):
    out = []
    for gi, d in enumerate(DILATIONS):
        m = s // d
        win = min(A_WIN, m)
        shifts = (0,) if m == QBLK else (0, A_SIDE, 2 * A_SIDE)
        t = _bias_tiles(rel_bias, gi * A_HEADS_PER_GROUP, A_HEADS_PER_GROUP, d, win, A_SIDE, shifts)
        out.append(_pair_lanes(t.reshape(len(shifts), 2, 2, QBLK, win)))
    return out


def _trunk(x, p):
    b, s, _ = x.shape
    xf = x.reshape(b * s, D_MODEL)
    g1, g2, g3, qb = _qkv_call(xf, p["g_attn"], p["w_in"], b, s)
    a = _attn_a_call((g1, g2, g3), _group_biases(p["rel_bias"], s), b, s)
    bo = _attn_b_call(qb, p["bias_b"], p["sink"], b, s)
    x1 = _post_call(xf, a, bo, p["g_attn"], p["w_in"], p["b_gate"], p["w_a_out"], p["w_b_out"], p["w_o"])
    y = _mlp_call(x1, s, p["g_ffn"], p["w_up"], p["conv_w"], p["conv_b"], p["w_down"], p["g_final"])
    return y.reshape(b, s, D_MODEL)


def kernel(x_prompt, x_sample, g_attn, w_in, b_gate, rel_bias, sink, w_a_out, w_b_out, w_o, g_ffn,
           w_up, conv_w, conv_b, w_down, g_final):
    assert w_in.shape[0] == 1, "one layer: the final RMSNorm is fused into its MLP kernel"
    p = _prep(w_in[0], b_gate[0], rel_bias, sink[0], w_a_out[0], w_b_out[0], w_o[0], w_up[0],
              conv_w[0], conv_b[0], w_down[0], g_attn[0], g_ffn[0], g_final)
    return (_trunk(x_prompt, p), _trunk(x_sample, p))
```

```python
import functools
import math

import numpy as np
import jax
import jax.numpy as jnp
from jax import lax
from jax.experimental import pallas as pl
from jax.experimental.pallas import tpu as pltpu

F32 = jnp.float32
BF16 = jnp.bfloat16

D_MODEL = 1024
HEAD_DIM = 64
DIL_PATTERNS = ((128, 1), (512, 4), (2048, 16))
DILATIONS = tuple(d for _, d in DIL_PATTERNS)
A_HEADS_PER_GROUP = 4
A_GROUP_WIDTH = A_HEADS_PER_GROUP * HEAD_DIM
N_GROUPS = len(DIL_PATTERNS)
A_HEADS = N_GROUPS * A_HEADS_PER_GROUP
A_WIDTH = A_HEADS * HEAD_DIM
A_SIDE = 64
B_Q_HEADS = 8
B_KV_HEADS = 2
B_GROUP = B_Q_HEADS // B_KV_HEADS
B_Q_WIDTH = B_Q_HEADS * HEAD_DIM
B_KV_WIDTH = B_KV_HEADS * HEAD_DIM
B_RADIUS = 128
N_BUCKETS = 32
MAX_DISTANCE = 1024
D_FF = 2816
EPS = 1e-6
NEG = -1e30
LOG2E = math.log2(math.e)
Q_SCALE = HEAD_DIM ** -0.5 * LOG2E

KA_OFF, VA_OFF = A_WIDTH, 2 * A_WIDTH
QB_OFF = 3 * A_WIDTH
KVB_OFF = QB_OFF + B_Q_WIDTH
GATE_OFF = KVB_OFF + 2 * B_KV_WIDTH
QKV_COLS = GATE_OFF

PAIR = 2 * HEAD_DIM
PAIR_COLS = 3 * PAIR
GRP_COLS = 2 * PAIR_COLS
B_COLS = B_Q_WIDTH + 4 * PAIR
QBLK = 128
A_WIN = QBLK + 2 * A_SIDE
B_WIN = QBLK + 2 * B_RADIUS
ROW_TILE = 512
STEP_TILES = 2
MLP_STEP_TILES = 2
FF_CHUNK = 256
HALO = 8
VMEM_LIMIT = 56 * 1024 * 1024


def _params(n_axes):
    return pltpu.CompilerParams(dimension_semantics=("arbitrary",) * n_axes,
                                vmem_limit_bytes=VMEM_LIMIT)


def _const_spec(shape, index=None):
    index = (0,) * len(shape) if index is None else index
    return pl.BlockSpec(shape, lambda *_: index, pipeline_mode=pl.Buffered(1))


def _rms(x, g):
    ms = jnp.mean(x * x, axis=-1, keepdims=True)
    return x * lax.rsqrt(ms + EPS) * g


def _lane_lo(rows=QBLK):
    return lax.broadcasted_iota(jnp.int32, (rows, PAIR), 1) < HEAD_DIM


def _pair_rhs(v):
    lo = _lane_lo(v.shape[0])
    zero = jnp.zeros_like(v)
    ones_lo = jnp.where(lo, 1.0, 0.0).astype(v.dtype)
    ones_hi = jnp.where(lo, 0.0, 1.0).astype(v.dtype)
    top = jnp.concatenate([jnp.where(lo, v, zero), ones_lo], axis=1)
    bot = jnp.concatenate([jnp.where(lo, zero, v), ones_hi], axis=1)
    return jnp.concatenate([top, bot], axis=0)


def _pair_scores(q, k):
    lo = _lane_lo(k.shape[0])
    zero = jnp.zeros_like(k)
    kbd = jnp.concatenate([jnp.where(lo, k, zero), jnp.where(lo, zero, k)], axis=0)
    return lax.dot_general(q, kbd, (((1,), (1,)), ((), ())), preferred_element_type=F32)


def _pair_softmax(sc, floors=(None, None)):
    win = sc.shape[1] // 2
    out = []
    for half, floor in enumerate(floors):
        s = sc[:, half * win:(half + 1) * win]
        m = jnp.broadcast_to(jnp.max(s, axis=-1, keepdims=True), (sc.shape[0], PAIR))
        if floor is not None:
            m = jnp.maximum(m, floor)
        out.append((m, jnp.exp2(s - jnp.concatenate([m] * (win // PAIR), axis=1))))
    return out


def _qkv_kernel(x_ref, g_ref, w_ref, o1_ref, o2_ref, o3_ref, ob_ref, tmp_ref):
    for sub in range(x_ref.shape[0] // ROW_TILE):
        _qkv_rows(sub, x_ref, g_ref, w_ref, (o1_ref, o2_ref, o3_ref), ob_ref, tmp_ref.at[sub])


def _qkv_rows(sub, x_ref, g_ref, w_ref, o_refs, ob_ref, tmp_ref):
    t = ROW_TILE
    rows = slice(sub * t, (sub + 1) * t)
    h = _rms(x_ref[rows, :], g_ref[...]).astype(BF16)

    def proj(c0, n):
        return jnp.dot(h, w_ref[:, c0:c0 + n], preferred_element_type=F32)

    for gi, (o_ref, d) in enumerate(zip(o_refs, DILATIONS)):
        c0 = gi * A_GROUP_WIDTH
        parts = (proj(c0, A_GROUP_WIDTH) * Q_SCALE, proj(KA_OFF + c0, A_GROUP_WIDTH),
                 proj(VA_OFF + c0, A_GROUP_WIDTH))
        for p in range(2):
            for j, part in enumerate(parts):
                c = p * 3 + j
                val = part[:, p * PAIR:(p + 1) * PAIR]
                if d == 1:
                    o_ref[0, rows, c * PAIR:(c + 1) * PAIR] = val.astype(BF16)
                else:
                    tmp_ref[c] = val
        if d > 1:
            td = t // d
            for r in range(d):
                for c in range(GRP_COLS // PAIR):
                    o_ref[r, sub * td:(sub + 1) * td, c * PAIR:(c + 1) * PAIR] = (
                        tmp_ref[c, pl.ds(r, td, stride=d), :].astype(BF16))

    ob_ref[rows, :B_Q_WIDTH] = (proj(QB_OFF, B_Q_WIDTH) * Q_SCALE).astype(BF16)
    kv = proj(KVB_OFF, 2 * B_KV_WIDTH)
    lo = _lane_lo(t)
    for j in range(2):
        one = kv[:, j * PAIR:(j + 1) * PAIR]
        swapped = pltpu.roll(one, HEAD_DIM, axis=1)
        c0 = B_Q_WIDTH + j * 2 * PAIR
        ob_ref[rows, c0:c0 + PAIR] = jnp.where(lo, one, swapped).astype(BF16)
        ob_ref[rows, c0 + PAIR:c0 + 2 * PAIR] = jnp.where(lo, swapped, one).astype(BF16)


def _qkv_call(x, g, w_in, b, s):
    n = x.shape[0]
    t = STEP_TILES * ROW_TILE
    tps = s // t
    grp_shape = lambda d: jax.ShapeDtypeStruct((b, d, s // d, GRP_COLS), BF16)
    grp_spec = lambda d: pl.BlockSpec((None, d, t // d, GRP_COLS), lambda i: (i // tps, 0, i % tps, 0))
    return pl.pallas_call(
        _qkv_kernel,
        grid=(n // t,),
        in_specs=[pl.BlockSpec((t, D_MODEL), lambda i: (i, 0)), _const_spec(g.shape),
                  _const_spec((D_MODEL, QKV_COLS))],
        out_specs=[grp_spec(d) for d in DILATIONS] + [pl.BlockSpec((t, B_COLS), lambda i: (i, 0))],
        out_shape=[grp_shape(d) for d in DILATIONS] + [jax.ShapeDtypeStruct((n, B_COLS), BF16)],
        scratch_shapes=[pltpu.VMEM((STEP_TILES, GRP_COLS // PAIR, ROW_TILE, PAIR), F32)],
        compiler_params=_params(1),
        name="qkv_proj",
    )(x, g, w_in)


def _window(i, nblk, m, win, side):
    if nblk == 1:
        return 0, 0
    start = pl.multiple_of(jnp.clip(i * QBLK - side, 0, m - win), 64)
    var = jnp.where(i == 0, 0, jnp.where(i == nblk - 1, 2, 1))
    return start, var


def _attn_a_kernel(g1_ref, g2_ref, g3_ref, b1_ref, b2_ref, b3_ref, out_ref, acc_ref, m_ref, l_ref, *, s):
    lo = _lane_lo()

    def attend(q, k, v, bias):
        (m_e, e_e), (m_o, e_o) = _pair_softmax(_pair_scores(q, k) + bias)
        e = jnp.concatenate([e_e, e_o], axis=1).astype(BF16)
        r = jnp.dot(e, _pair_rhs(v), preferred_element_type=F32)
        return r[:, :PAIR], jnp.where(lo, m_e, m_o), r[:, PAIR:]

    order = sorted(range(N_GROUPS), key=lambda gi: -DILATIONS[gi])
    assert DILATIONS[order[-1]] == 1
    refs, bias_refs = (g1_ref, g2_ref, g3_ref), (b1_ref, b2_ref, b3_ref)
    for step, gi in enumerate(order):
        ref, bias_ref, d = refs[gi], bias_refs[gi], DILATIONS[gi]
        m = s // d
        nblk = m // QBLK
        win = min(A_WIN, m)
        shift = nblk.bit_length() - 1
        assert nblk == 1 << shift

        def body(j, carry, ref=ref, bias_ref=bias_ref, d=d, m=m, nblk=nblk, win=win, shift=shift, step=step):
            r, i = (0, j) if d == 1 else (j >> shift, j & (nblk - 1))
            q0 = pl.multiple_of(i * QBLK, QBLK)
            start, var = _window(i, nblk, m, win, A_SIDE)
            o, mm, ll = attend(ref[r, pl.ds(q0, QBLK), :PAIR], ref[r, pl.ds(start, win), PAIR:2 * PAIR],
                               ref[r, pl.ds(start, win), 2 * PAIR:], bias_ref[var])
            rows = pl.ds(q0, QBLK) if d == 1 else pl.ds(r + d * q0, QBLK, stride=d)
            if step == 0:
                acc_ref[rows, :] = o
                m_ref[rows, :] = mm
                l_ref[rows, :] = ll
                return carry
            m_old = m_ref[rows, :]
            m_new = jnp.maximum(m_old, mm)
            alpha = jnp.exp2(m_old - m_new)
            beta = jnp.exp2(mm - m_new)
            l_new = alpha * l_ref[rows, :] + beta * ll
            acc_new = alpha * acc_ref[rows, :] + beta * o
            if step == N_GROUPS - 1:
                out_ref[rows, :] = (acc_new / l_new).astype(BF16)
            else:
                m_ref[rows, :] = m_new
                l_ref[rows, :] = l_new
                acc_ref[rows, :] = acc_new
            return carry

        lax.fori_loop(0, d * nblk, body, 0, unroll=8)


def _attn_a_call(groups, biases, b, s):
    in_specs = [pl.BlockSpec((None, d, s // d, PAIR_COLS), lambda bi, p: (bi, 0, 0, p)) for d in DILATIONS]
    in_specs += [pl.BlockSpec((bias.shape[0], None) + bias.shape[2:], lambda bi, p: (0, p, 0, 0))
                 for bias in biases]
    out = pl.pallas_call(
        functools.partial(_attn_a_kernel, s=s),
        grid=(b, 2),
        in_specs=in_specs,
        out_specs=pl.BlockSpec((None, s, PAIR), lambda bi, p: (bi, 0, p)),
        out_shape=jax.ShapeDtypeStruct((b, s, A_GROUP_WIDTH), BF16),
        scratch_shapes=[pltpu.VMEM((s, PAIR), F32)] * 3,
        compiler_params=_params(2),
        name="attn_dil",
    )(*groups, *biases)
    return out.reshape(b * s, A_GROUP_WIDTH)


def _attn_b_kernel(q_ref, k_ref, v_ref, bias_ref, sink_ref, o_ref, *, s):
    nblk = s // QBLK
    lo = _lane_lo()
    c = pl.program_id(1)

    def block(i, carry):
        q0 = pl.multiple_of(i * QBLK, QBLK)
        start, var = _window(i, nblk, s, B_WIN, B_RADIUS)
        q = jnp.concatenate([q_ref[pl.ds(q0, QBLK), :PAIR], q_ref[pl.ds(q0, QBLK), PAIR:]], axis=0)
        sc = _pair_scores(q, k_ref[pl.ds(start, B_WIN), :])
        rhs = _pair_rhs(v_ref[pl.ds(start, B_WIN), :])
        es, sinks = [], []
        for p in range(2):
            sk_e = sink_ref[c * B_GROUP + 2 * p]
            sk_o = sink_ref[c * B_GROUP + 2 * p + 1]
            (m_e, e_e), (m_o, e_o) = _pair_softmax(sc[p * QBLK:(p + 1) * QBLK] + bias_ref[var, p], (sk_e, sk_o))
            es.append(jnp.concatenate([e_e, e_o], axis=1).astype(BF16))
            sinks.append(jnp.where(lo, jnp.exp2(sk_e - m_e), jnp.exp2(sk_o - m_o)))
        r = jnp.dot(jnp.concatenate(es, axis=0), rhs, preferred_element_type=F32)
        for p in range(2):
            rp = r[p * QBLK:(p + 1) * QBLK]
            o_ref[pl.ds(q0, QBLK), p * PAIR:(p + 1) * PAIR] = (rp[:, :PAIR] / (rp[:, PAIR:] + sinks[p])).astype(BF16)
        return carry

    lax.fori_loop(0, nblk, block, 0, unroll=8)


def _attn_b_call(qkv_b, bias, sink, b, s):
    qkv_v = qkv_b.reshape(b, s, B_COLS)
    qw = B_GROUP * HEAD_DIM
    kv0 = B_Q_WIDTH // PAIR
    o = pl.pallas_call(
        functools.partial(_attn_b_kernel, s=s),
        grid=(b, B_KV_HEADS),
        in_specs=[pl.BlockSpec((None, s, qw), lambda bi, c: (bi, 0, c)),
                  pl.BlockSpec((None, s, PAIR), lambda bi, c: (bi, 0, kv0 + c)),
                  pl.BlockSpec((None, s, PAIR), lambda bi, c: (bi, 0, kv0 + B_KV_HEADS + c)),
                  pl.BlockSpec((None,) + bias.shape[1:], lambda bi, c: (c, 0, 0, 0, 0)),
                  pl.BlockSpec(memory_space=pltpu.SMEM)],
        out_specs=pl.BlockSpec((None, s, qw), lambda bi, c: (bi, 0, c)),
        out_shape=jax.ShapeDtypeStruct((b, s, B_Q_WIDTH), BF16),
        compiler_params=_params(2),
        name="attn_win",
    )(qkv_v, qkv_v, qkv_v, bias, sink)
    return o.reshape(b * s, B_Q_WIDTH)


def _post_kernel(x_ref, a_ref, bo_ref, g_ref, wga_ref, wgb_ref, bg_ref, wa_ref, wb_ref, wo_ref,
                 out_ref, mix_ref):
    cw = 256
    for sub in range(x_ref.shape[0] // ROW_TILE):
        rows = slice(sub * ROW_TILE, (sub + 1) * ROW_TILE)
        x = x_ref[rows, :]
        h = _rms(x, g_ref[...]).astype(BF16)
        a = a_ref[rows, :]
        bo = bo_ref[rows, :]
        for c0 in range(0, D_MODEL, cw):
            cs = slice(c0, c0 + cw)
            ap = jnp.dot(a, wa_ref[:, cs], preferred_element_type=F32)
            bp = jnp.dot(bo, wb_ref[:, cs], preferred_element_type=F32)
            ga = jnp.dot(h, wga_ref[:, cs], preferred_element_type=F32) + bg_ref[:, cs]
            gb = (jnp.dot(h, wgb_ref[:, cs], preferred_element_type=F32)
                  + bg_ref[:, D_MODEL + c0:D_MODEL + c0 + cw])
            mix_ref[rows, cs] = (jax.nn.sigmoid(ga) * ap + jax.nn.sigmoid(gb) * bp).astype(BF16)
        out_ref[rows, :] = x + jnp.dot(mix_ref[rows, :], wo_ref[...], preferred_element_type=F32)


def _post_call(x, a, bo, g, w_in, bg, wa, wb, wo):
    n = x.shape[0]
    t = STEP_TILES * ROW_TILE
    rows = lambda w: pl.BlockSpec((t, w), lambda i: (i, 0))
    gate_blk = GATE_OFF // D_MODEL
    return pl.pallas_call(
        _post_kernel,
        grid=(n // t,),
        in_specs=[rows(D_MODEL), rows(A_GROUP_WIDTH), rows(B_Q_WIDTH), _const_spec(g.shape),
                  _const_spec((D_MODEL, D_MODEL), (0, gate_blk)),
                  _const_spec((D_MODEL, D_MODEL), (0, gate_blk + 1)),
                  _const_spec(bg.shape), _const_spec(wa.shape), _const_spec(wb.shape), _const_spec(wo.shape)],
        out_specs=rows(D_MODEL),
        out_shape=jax.ShapeDtypeStruct((n, D_MODEL), F32),
        scratch_shapes=[pltpu.VMEM((t, D_MODEL), BF16)],
        compiler_params=_params(1),
        name="post_attn",
    )(x, a, bo, g, w_in, w_in, bg, wa, wb, wo)


def _gelu_tanh(x):
    k = -2.0 * LOG2E * math.sqrt(2.0 / math.pi)
    return x / (1.0 + jnp.exp2(x * (k + (k * 0.044715) * (x * x))))


def _mlp_kernel(xp_ref, x_ref, xn_ref, g_ref, wup_ref, cw_ref, cb_ref, wdn_ref, gf_ref, out_ref,
                act_ref, perm_ref, *, steps_per_seq):
    n_sub = x_ref.shape[0] // ROW_TILE
    i = pl.program_id(0)
    keep_prev = jnp.where((i % steps_per_seq) == 0, 0.0, 1.0)
    keep_next = jnp.where((i % steps_per_seq) == steps_per_seq - 1, 0.0, 1.0)
    g = g_ref[...]
    for sub in range(n_sub):
        r0 = sub * ROW_TILE
        halo_prev = (_rms(xp_ref[...], g) * keep_prev if sub == 0
                     else _rms(x_ref[r0 - HALO:r0, :], g))
        halo_next = (_rms(xn_ref[...], g) * keep_next if sub == n_sub - 1
                     else _rms(x_ref[r0 + ROW_TILE:r0 + ROW_TILE + HALO, :], g))
        _mlp_rows(x_ref.at[r0:r0 + ROW_TILE], halo_prev, halo_next, g, wup_ref, cw_ref, cb_ref, wdn_ref,
                  gf_ref, out_ref.at[r0:r0 + ROW_TILE], act_ref.at[sub], perm_ref.at[sub])


def _mlp_rows(x_ref, halo_prev, halo_next, g, wup_ref, cw_ref, cb_ref, wdn_ref, gf_ref, out_ref,
              act_ref, perm_ref):
    t = ROW_TILE
    nv = t // HALO
    ncb = D_MODEL // PAIR
    x = x_ref[...]
    h = _rms(x, g)
    for r in range(HALO):
        for c in range(ncb):
            perm_ref[c, pl.ds(r, nv, stride=HALO), :] = h[r * nv:(r + 1) * nv, c * PAIR:(c + 1) * PAIR]
    hp = jnp.concatenate([perm_ref[c] for c in range(ncb)], axis=1)
    he = jnp.concatenate([hp, halo_prev, halo_next], axis=0).astype(BF16)

    def conv(u, c0):
        n = u.shape[1]
        w = cw_ref[:, c0:c0 + n]
        sub = lax.broadcasted_iota(jnp.int32, (HALO, n), 0)
        main, halo_prev, halo_next = u[:t], u[t:t + HALO], u[t + HALO:]
        head = jnp.where(sub == 0, halo_prev[HALO - 1:], pltpu.roll(main[t - HALO:], 1, axis=0))
        tail = jnp.where(sub == HALO - 1, halo_next[:1], pltpu.roll(main[:HALO], HALO - 1, axis=0))
        prev = jnp.concatenate([head, main[:t - HALO]], axis=0)
        nxt = jnp.concatenate([main[HALO:], tail], axis=0)
        return prev * w[0:1] + main * w[1:2] + nxt * w[2:3] + cb_ref[:, c0:c0 + n]

    for c0 in range(0, D_FF, FF_CHUNK):
        c1 = min(c0 + FF_CHUNK, D_FF)
        ug = jnp.dot(he, wup_ref[:, c0:c1], preferred_element_type=F32)
        uv = jnp.dot(he, wup_ref[:, D_FF + c0:D_FF + c1], preferred_element_type=F32)
        act_ref[:, c0:c1] = (_gelu_tanh(conv(ug, c0)) * conv(uv, D_FF + c0)).astype(BF16)
    yp = jnp.dot(act_ref[...], wdn_ref[...], preferred_element_type=F32)
    for c in range(ncb):
        perm_ref[c] = yp[:, c * PAIR:(c + 1) * PAIR]
    y = x + jnp.concatenate(
        [jnp.concatenate([perm_ref[c, pl.ds(r, nv, stride=HALO), :] for c in range(ncb)], axis=1)
         for r in range(HALO)], axis=0)
    out_ref[...] = _rms(y, gf_ref[...])


def _mlp_call(x, s, g, wup, cw, cb, wdn, gf):
    n = x.shape[0]
    t = MLP_STEP_TILES * ROW_TILE
    hb = t // HALO
    nhb = n // HALO
    consts = (g, wup, cw, cb, wdn, gf)
    return pl.pallas_call(
        functools.partial(_mlp_kernel, steps_per_seq=s // t),
        grid=(n // t,),
        in_specs=[pl.BlockSpec((HALO, D_MODEL), lambda i: (jnp.maximum(i * hb - 1, 0), 0)),
                  pl.BlockSpec((t, D_MODEL), lambda i: (i, 0)),
                  pl.BlockSpec((HALO, D_MODEL), lambda i: (jnp.minimum((i + 1) * hb, nhb - 1), 0))]
        + [_const_spec(c.shape) for c in consts],
        out_specs=pl.BlockSpec((t, D_MODEL), lambda i: (i, 0)),
        out_shape=jax.ShapeDtypeStruct((n, D_MODEL), F32),
        scratch_shapes=[pltpu.VMEM((MLP_STEP_TILES, ROW_TILE, D_FF), BF16),
                        pltpu.VMEM((MLP_STEP_TILES, D_MODEL // PAIR, ROW_TILE, PAIR), F32)],
        compiler_params=_params(1),
        name="conv_mlp",
    )(x, x, x, *consts)


def _rel_bucket(rel):
    nb = N_BUCKETS // 2
    max_exact = nb // 2
    rel = np.asarray(rel, np.int32)
    ret = np.where(rel > 0, nb, 0)
    n = np.abs(rel)
    nf = np.maximum(n, 1).astype(np.float32)
    large = max_exact + (np.log(nf / np.float32(max_exact)) / np.float32(math.log(MAX_DISTANCE / max_exact))
                         * np.float32(nb - max_exact)).astype(np.int32)
    large = np.minimum(large, nb - 1)
    return (ret + np.where(n < max_exact, n, large)).astype(np.int32)


def _bias_tiles(rel_bias, head0, n_heads, dilation, win, side, shifts):
    offs = np.arange(-side, side + 1) * dilation
    line = rel_bias[_rel_bucket(offs)][:, head0:head0 + n_heads].T.astype(F32) * LOG2E
    k0 = QBLK + max(shifts) - side
    p = win + k0 + side + 1
    padded = jnp.pad(line, ((0, 0), (k0, p - k0 - (2 * side + 1))), constant_values=NEG)
    skew = jnp.tile(padded, (1, QBLK + 1))[:, :QBLK * (p - 1)].reshape(n_heads, QBLK, p - 1)
    return jnp.stack([skew[:, :, k0 - sh + side:k0 - sh + side + win] for sh in shifts], 0)


def _prep(w_in, b_gate, rel_bias, sink, w_a_out, w_b_out, w_o, w_up, conv_w, conv_b, w_down,
          g_attn, g_ffn, g_final):
    row = lambda v: v.reshape(1, -1).astype(F32)
    p = dict(
        w_in=w_in.astype(BF16), b_gate=row(b_gate),
        w_a_out=w_a_out.astype(BF16), w_b_out=w_b_out.astype(BF16), w_o=w_o.astype(BF16),
        w_up=w_up.astype(BF16), conv_w=conv_w.astype(F32), conv_b=row(conv_b),
        w_down=w_down.astype(BF16), g_attn=row(g_attn), g_ffn=row(g_ffn), g_final=row(g_final),
        rel_bias=rel_bias,
    )
    bb = _bias_tiles(rel_bias, A_HEADS, B_Q_HEADS, 1, B_WIN, B_RADIUS, (0, B_RADIUS, 2 * B_RADIUS))
    p["bias_b"] = _pair_lanes(bb.reshape(3, B_KV_HEADS, 2, 2, QBLK, B_WIN)).transpose(1, 0, 2, 3, 4)
    p["sink"] = sink.astype(F32) * LOG2E
    return p


def _pair_lanes(t):
    return jnp.concatenate([t[..., 0, :, :], t[..., 1, :, :]], axis=-1)


def _group_biases(rel_bias, s):
    out = []
    for gi, d in enumerate(DILATIONS):
        m = s // d
        win = min(A_WIN, m)
        shifts = (0,) if m == QBLK else (0, A_SIDE, 2 * A_SIDE)
        t = _bias_tiles(rel_bias, gi * A_HEADS_PER_GROUP, A_HEADS_PER_GROUP, d, win, A_SIDE, shifts)
        out.append(_pair_lanes(t.reshape(len(shifts), 2, 2, QBLK, win)))
    return out


def _trunk(x, p):
    b, s, _ = x.shape
    xf = x.reshape(b * s, D_MODEL)
    g1, g2, g3, qb = _qkv_call(xf, p["g_attn"], p["w_in"], b, s)
    a = _attn_a_call((g1, g2, g3), _group_biases(p["rel_bias"], s), b, s)
    bo = _attn_b_call(qb, p["bias_b"], p["sink"], b, s)
    x1 = _post_call(xf, a, bo, p["g_attn"], p["w_in"], p["b_gate"], p["w_a_out"], p["w_b_out"], p["w_o"])
    y = _mlp_call(x1, s, p["g_ffn"], p["w_up"], p["conv_w"], p["conv_b"], p["w_down"], p["g_final"])
    return y.reshape(b, s, D_MODEL)


def kernel(x_prompt, x_sample, g_attn, w_in, b_gate, rel_bias, sink, w_a_out, w_b_out, w_o, g_ffn,
           w_up, conv_w, conv_b, w_down, g_final):
    assert w_in.shape[0] == 1, "one layer: the final RMSNorm is fused into its MLP kernel"
    p = _prep(w_in[0], b_gate[0], rel_bias, sink[0], w_a_out[0], w_b_out[0], w_o[0], w_up[0],
              conv_w[0], conv_b[0], w_down[0], g_attn[0], g_ffn[0], g_final)
    return (_trunk(x_prompt, p), _trunk(x_sample, p))
```

```python
import functools
import math

import numpy as np
import jax
import jax.numpy as jnp
from jax import lax
from jax.experimental import pallas as pl
from jax.experimental.pallas import tpu as pltpu

F32 = jnp.float32
BF16 = jnp.bfloat16

D_MODEL = 1024
HEAD_DIM = 64
DIL_PATTERNS = ((128, 1), (512, 4), (2048, 16))
DILATIONS = tuple(d for _, d in DIL_PATTERNS)
A_HEADS_PER_GROUP = 4
A_GROUP_WIDTH = A_HEADS_PER_GROUP * HEAD_DIM
N_GROUPS = len(DIL_PATTERNS)
A_HEADS = N_GROUPS * A_HEADS_PER_GROUP
A_WIDTH = A_HEADS * HEAD_DIM
A_SIDE = 64
B_Q_HEADS = 8
B_KV_HEADS = 2
B_GROUP = B_Q_HEADS // B_KV_HEADS
B_Q_WIDTH = B_Q_HEADS * HEAD_DIM
B_KV_WIDTH = B_KV_HEADS * HEAD_DIM
B_RADIUS = 128
N_BUCKETS = 32
MAX_DISTANCE = 1024
D_FF = 2816
EPS = 1e-6
NEG = -1e30
LOG2E = math.log2(math.e)
Q_SCALE = HEAD_DIM ** -0.5 * LOG2E

KA_OFF, VA_OFF = A_WIDTH, 2 * A_WIDTH
QB_OFF = 3 * A_WIDTH
KVB_OFF = QB_OFF + B_Q_WIDTH
GATE_OFF = KVB_OFF + 2 * B_KV_WIDTH
QKV_COLS = GATE_OFF

PAIR = 2 * HEAD_DIM
PAIR_COLS = 3 * PAIR
GRP_COLS = 2 * PAIR_COLS
B_COLS = B_Q_WIDTH + 4 * PAIR
QBLK = 128
A_WIN = QBLK + 2 * A_SIDE
B_WIN = QBLK + 2 * B_RADIUS
ROW_TILE = 512
STEP_TILES = 2
MLP_STEP_TILES = 2
MXU_COLS = 256
FF_CHUNK = MXU_COLS
HALO = 8
VMEM_LIMIT = 56 * 1024 * 1024


def _params(n_axes):
    return pltpu.CompilerParams(dimension_semantics=("arbitrary",) * n_axes,
                                vmem_limit_bytes=VMEM_LIMIT)


def _const_spec(shape, index=None):
    index = (0,) * len(shape) if index is None else index
    return pl.BlockSpec(shape, lambda *_: index, pipeline_mode=pl.Buffered(1))


def _rms(x, g):
    ms = jnp.mean(x * x, axis=-1, keepdims=True)
    return x * lax.rsqrt(ms + EPS) * g


def _lane_lo(rows=QBLK):
    return lax.broadcasted_iota(jnp.int32, (rows, PAIR), 1) < HEAD_DIM


def _pair_rhs(v):
    lo = _lane_lo(v.shape[0])
    zero = jnp.zeros_like(v)
    ones_lo = jnp.where(lo, 1.0, 0.0).astype(v.dtype)
    ones_hi = jnp.where(lo, 0.0, 1.0).astype(v.dtype)
    top = jnp.concatenate([jnp.where(lo, v, zero), ones_lo], axis=1)
    bot = jnp.concatenate([jnp.where(lo, zero, v), ones_hi], axis=1)
    return jnp.concatenate([top, bot], axis=0)


def _pair_scores(q, k):
    lo = _lane_lo(k.shape[0])
    zero = jnp.zeros_like(k)
    kbd = jnp.concatenate([jnp.where(lo, k, zero), jnp.where(lo, zero, k)], axis=0)
    return lax.dot_general(q, kbd, (((1,), (1,)), ((), ())), preferred_element_type=F32)


def _pair_softmax(sc, floors=(None, None)):
    win = sc.shape[1] // 2
    out = []
    for half, floor in enumerate(floors):
        s = sc[:, half * win:(half + 1) * win]
        m = jnp.broadcast_to(jnp.max(s, axis=-1, keepdims=True), (sc.shape[0], PAIR))
        if floor is not None:
            m = jnp.maximum(m, floor)
        out.append((m, jnp.exp2(s - jnp.concatenate([m] * (win // PAIR), axis=1))))
    return out


def _qkv_kernel(x_ref, g_ref, w_ref, o1_ref, o2_ref, o3_ref, ob_ref, tmp_ref):
    for sub in range(x_ref.shape[0] // ROW_TILE):
        _qkv_rows(sub, x_ref, g_ref, w_ref, (o1_ref, o2_ref, o3_ref), ob_ref, tmp_ref.at[sub])


def _qkv_rows(sub, x_ref, g_ref, w_ref, o_refs, ob_ref, tmp_ref):
    t = ROW_TILE
    rows = slice(sub * t, (sub + 1) * t)
    h = _rms(x_ref[rows, :], g_ref[...]).astype(BF16)

    def proj(c0, n):
        return jnp.dot(h, w_ref[:, c0:c0 + n], preferred_element_type=F32)

    for gi, (o_ref, d) in enumerate(zip(o_refs, DILATIONS)):
        c0 = gi * A_GROUP_WIDTH
        parts = (proj(c0, A_GROUP_WIDTH) * Q_SCALE, proj(KA_OFF + c0, A_GROUP_WIDTH),
                 proj(VA_OFF + c0, A_GROUP_WIDTH))
        for p in range(2):
            for j, part in enumerate(parts):
                c = p * 3 + j
                val = part[:, p * PAIR:(p + 1) * PAIR]
                if d == 1:
                    o_ref[0, rows, c * PAIR:(c + 1) * PAIR] = val.astype(BF16)
                else:
                    tmp_ref[c] = val
        if d > 1:
            td = t // d
            for r in range(d):
                for c in range(GRP_COLS // PAIR):
                    o_ref[r, sub * td:(sub + 1) * td, c * PAIR:(c + 1) * PAIR] = (
                        tmp_ref[c, pl.ds(r, td, stride=d), :].astype(BF16))

    ob_ref[rows, :B_Q_WIDTH] = (proj(QB_OFF, B_Q_WIDTH) * Q_SCALE).astype(BF16)
    kv = proj(KVB_OFF, 2 * B_KV_WIDTH)
    lo = _lane_lo(t)
    for j in range(2):
        one = kv[:, j * PAIR:(j + 1) * PAIR]
        swapped = pltpu.roll(one, HEAD_DIM, axis=1)
        c0 = B_Q_WIDTH + j * 2 * PAIR
        ob_ref[rows, c0:c0 + PAIR] = jnp.where(lo, one, swapped).astype(BF16)
        ob_ref[rows, c0 + PAIR:c0 + 2 * PAIR] = jnp.where(lo, swapped, one).astype(BF16)


def _qkv_call(x, g, w_in, b, s):
    n = x.shape[0]
    t = STEP_TILES * ROW_TILE
    tps = s // t
    grp_shape = lambda d: jax.ShapeDtypeStruct((b, d, s // d, GRP_COLS), BF16)
    grp_spec = lambda d: pl.BlockSpec((None, d, t // d, GRP_COLS), lambda i: (i // tps, 0, i % tps, 0))
    return pl.pallas_call(
        _qkv_kernel,
        grid=(n // t,),
        in_specs=[pl.BlockSpec((t, D_MODEL), lambda i: (i, 0)), _const_spec(g.shape),
                  _const_spec((D_MODEL, QKV_COLS))],
        out_specs=[grp_spec(d) for d in DILATIONS] + [pl.BlockSpec((t, B_COLS), lambda i: (i, 0))],
        out_shape=[grp_shape(d) for d in DILATIONS] + [jax.ShapeDtypeStruct((n, B_COLS), BF16)],
        scratch_shapes=[pltpu.VMEM((STEP_TILES, GRP_COLS // PAIR, ROW_TILE, PAIR), F32)],
        compiler_params=_params(1),
        name="qkv_proj",
    )(x, g, w_in)


def _window(i, nblk, m, win, side):
    if nblk == 1:
        return 0, 0
    start = pl.multiple_of(jnp.clip(i * QBLK - side, 0, m - win), 64)
    var = jnp.where(i == 0, 0, jnp.where(i == nblk - 1, 2, 1))
    return start, var


def _attn_a_kernel(g1_ref, g2_ref, g3_ref, b1_ref, b2_ref, b3_ref, out_ref, acc_ref, m_ref, l_ref, *, s):
    lo = _lane_lo()

    def attend(q, k, v, bias):
        (m_e, e_e), (m_o, e_o) = _pair_softmax(_pair_scores(q, k) + bias)
        e = jnp.concatenate([e_e, e_o], axis=1).astype(BF16)
        r = jnp.dot(e, _pair_rhs(v), preferred_element_type=F32)
        return r[:, :PAIR], jnp.where(lo, m_e, m_o), r[:, PAIR:]

    order = sorted(range(N_GROUPS), key=lambda gi: -DILATIONS[gi])
    assert DILATIONS[order[-1]] == 1
    refs, bias_refs = (g1_ref, g2_ref, g3_ref), (b1_ref, b2_ref, b3_ref)
    for step, gi in enumerate(order):
        ref, bias_ref, d = refs[gi], bias_refs[gi], DILATIONS[gi]
        m = s // d
        nblk = m // QBLK
        win = min(A_WIN, m)
        shift = nblk.bit_length() - 1
        assert nblk == 1 << shift

        def body(j, carry, ref=ref, bias_ref=bias_ref, d=d, m=m, nblk=nblk, win=win, shift=shift, step=step):
            r, i = (0, j) if d == 1 else (j >> shift, j & (nblk - 1))
            q0 = pl.multiple_of(i * QBLK, QBLK)
            start, var = _window(i, nblk, m, win, A_SIDE)
            o, mm, ll = attend(ref[r, pl.ds(q0, QBLK), :PAIR], ref[r, pl.ds(start, win), PAIR:2 * PAIR],
                               ref[r, pl.ds(start, win), 2 * PAIR:], bias_ref[var])
            rows = pl.ds(q0, QBLK) if d == 1 else pl.ds(r + d * q0, QBLK, stride=d)
            if step == 0:
                acc_ref[rows, :] = o
                m_ref[rows, :] = mm
                l_ref[rows, :] = ll
                return carry
            m_old = m_ref[rows, :]
            m_new = jnp.maximum(m_old, mm)
            alpha = jnp.exp2(m_old - m_new)
            beta = jnp.exp2(mm - m_new)
            l_new = alpha * l_ref[rows, :] + beta * ll
            acc_new = alpha * acc_ref[rows, :] + beta * o
            if step == N_GROUPS - 1:
                out_ref[rows, :] = (acc_new / l_new).astype(BF16)
            else:
                m_ref[rows, :] = m_new
                l_ref[rows, :] = l_new
                acc_ref[rows, :] = acc_new
            return carry

        lax.fori_loop(0, d * nblk, body, 0, unroll=8)


def _attn_a_call(groups, biases, b, s):
    in_specs = [pl.BlockSpec((None, d, s // d, PAIR_COLS), lambda p, bi: (bi, 0, 0, p)) for d in DILATIONS]
    in_specs += [pl.BlockSpec((bias.shape[0], None) + bias.shape[2:], lambda p, bi: (0, p, 0, 0))
                 for bias in biases]
    out = pl.pallas_call(
        functools.partial(_attn_a_kernel, s=s),
        grid=(2, b),
        in_specs=in_specs,
        out_specs=pl.BlockSpec((None, s, PAIR), lambda p, bi: (bi, 0, p)),
        out_shape=jax.ShapeDtypeStruct((b, s, A_GROUP_WIDTH), BF16),
        scratch_shapes=[pltpu.VMEM((s, PAIR), F32)] * 3,
        compiler_params=_params(2),
        name="attn_dil",
    )(*groups, *biases)
    return out.reshape(b * s, A_GROUP_WIDTH)


def _attn_b_kernel(q_ref, k_ref, v_ref, bias_ref, sink_ref, o_ref, *, s):
    nblk = s // QBLK
    lo = _lane_lo()
    c = pl.program_id(0)

    def block(i, carry):
        q0 = pl.multiple_of(i * QBLK, QBLK)
        start, var = _window(i, nblk, s, B_WIN, B_RADIUS)
        q = jnp.concatenate([q_ref[pl.ds(q0, QBLK), :PAIR], q_ref[pl.ds(q0, QBLK), PAIR:]], axis=0)
        sc = _pair_scores(q, k_ref[pl.ds(start, B_WIN), :])
        rhs = _pair_rhs(v_ref[pl.ds(start, B_WIN), :])
        es, sinks = [], []
        for p in range(2):
            sk_e = sink_ref[c * B_GROUP + 2 * p]
            sk_o = sink_ref[c * B_GROUP + 2 * p + 1]
            (m_e, e_e), (m_o, e_o) = _pair_softmax(sc[p * QBLK:(p + 1) * QBLK] + bias_ref[var, p], (sk_e, sk_o))
            es.append(jnp.concatenate([e_e, e_o], axis=1).astype(BF16))
            sinks.append(jnp.where(lo, jnp.exp2(sk_e - m_e), jnp.exp2(sk_o - m_o)))
        r = jnp.dot(jnp.concatenate(es, axis=0), rhs, preferred_element_type=F32)
        for p in range(2):
            rp = r[p * QBLK:(p + 1) * QBLK]
            o_ref[pl.ds(q0, QBLK), p * PAIR:(p + 1) * PAIR] = (rp[:, :PAIR] / (rp[:, PAIR:] + sinks[p])).astype(BF16)
        return carry

    lax.fori_loop(0, nblk, block, 0, unroll=8)


def _attn_b_call(qkv_b, bias, sink, b, s):
    qkv_v = qkv_b.reshape(b, s, B_COLS)
    qw = B_GROUP * HEAD_DIM
    kv0 = B_Q_WIDTH // PAIR
    o = pl.pallas_call(
        functools.partial(_attn_b_kernel, s=s),
        grid=(B_KV_HEADS, b),
        in_specs=[pl.BlockSpec((None, s, qw), lambda c, bi: (bi, 0, c)),
                  pl.BlockSpec((None, s, PAIR), lambda c, bi: (bi, 0, kv0 + c)),
                  pl.BlockSpec((None, s, PAIR), lambda c, bi: (bi, 0, kv0 + B_KV_HEADS + c)),
                  pl.BlockSpec((None,) + bias.shape[1:], lambda c, bi: (c, 0, 0, 0, 0)),
                  pl.BlockSpec(memory_space=pltpu.SMEM)],
        out_specs=pl.BlockSpec((None, s, qw), lambda c, bi: (bi, 0, c)),
        out_shape=jax.ShapeDtypeStruct((b, s, B_Q_WIDTH), BF16),
        compiler_params=_params(2),
        name="attn_win",
    )(qkv_v, qkv_v, qkv_v, bias, sink)
    return o.reshape(b * s, B_Q_WIDTH)


def _post_kernel(x_ref, a_ref, bo_ref, g_ref, wga_ref, wgb_ref, bg_ref, wa_ref, wb_ref, wo_ref,
                 out_ref, mix_ref):
    for sub in range(x_ref.shape[0] // ROW_TILE):
        rows = slice(sub * ROW_TILE, (sub + 1) * ROW_TILE)
        x = x_ref[rows, :]
        h = _rms(x, g_ref[...]).astype(BF16)
        a = a_ref[rows, :]
        bo = bo_ref[rows, :]
        for c0 in range(0, D_MODEL, MXU_COLS):
            cs = slice(c0, c0 + MXU_COLS)
            ap = jnp.dot(a, wa_ref[:, cs], preferred_element_type=F32)
            bp = jnp.dot(bo, wb_ref[:, cs], preferred_element_type=F32)
            ga = jnp.dot(h, wga_ref[:, cs], preferred_element_type=F32) + bg_ref[:, cs]
            gb = (jnp.dot(h, wgb_ref[:, cs], preferred_element_type=F32)
                  + bg_ref[:, D_MODEL + c0:D_MODEL + c0 + MXU_COLS])
            mix_ref[rows, cs] = (jax.nn.sigmoid(ga) * ap + jax.nn.sigmoid(gb) * bp).astype(BF16)
        out_ref[rows, :] = x + jnp.dot(mix_ref[rows, :], wo_ref[...], preferred_element_type=F32)


def _post_call(x, a, bo, g, w_in, bg, wa, wb, wo):
    n = x.shape[0]
    t = STEP_TILES * ROW_TILE
    rows = lambda w: pl.BlockSpec((t, w), lambda i: (i, 0))
    gate_blk = GATE_OFF // D_MODEL
    return pl.pallas_call(
        _post_kernel,
        grid=(n // t,),
        in_specs=[rows(D_MODEL), rows(A_GROUP_WIDTH), rows(B_Q_WIDTH), _const_spec(g.shape),
                  _const_spec((D_MODEL, D_MODEL), (0, gate_blk)),
                  _const_spec((D_MODEL, D_MODEL), (0, gate_blk + 1)),
                  _const_spec(bg.shape), _const_spec(wa.shape), _const_spec(wb.shape), _const_spec(wo.shape)],
        out_specs=rows(D_MODEL),
        out_shape=jax.ShapeDtypeStruct((n, D_MODEL), F32),
        scratch_shapes=[pltpu.VMEM((t, D_MODEL), BF16)],
        compiler_params=_params(1),
        name="post_attn",
    )(x, a, bo, g, w_in, w_in, bg, wa, wb, wo)


def _gelu_tanh(x):
    k = -2.0 * LOG2E * math.sqrt(2.0 / math.pi)
    return x / (1.0 + jnp.exp2(x * (k + (k * 0.044715) * (x * x))))


def _mlp_kernel(xp_ref, x_ref, xn_ref, g_ref, wup_ref, cw_ref, cb_ref, wdn_ref, gf_ref, out_ref,
                act_ref, perm_ref, *, steps_per_seq):
    n_sub = x_ref.shape[0] // ROW_TILE
    i = pl.program_id(0)
    keep_prev = jnp.where((i % steps_per_seq) == 0, 0.0, 1.0)
    keep_next = jnp.where((i % steps_per_seq) == steps_per_seq - 1, 0.0, 1.0)
    g = g_ref[...]
    for sub in range(n_sub):
        r0 = sub * ROW_TILE
        halo_prev = (_rms(xp_ref[...], g) * keep_prev if sub == 0
                     else _rms(x_ref[r0 - HALO:r0, :], g))
        halo_next = (_rms(xn_ref[...], g) * keep_next if sub == n_sub - 1
                     else _rms(x_ref[r0 + ROW_TILE:r0 + ROW_TILE + HALO, :], g))
        _mlp_rows(x_ref.at[r0:r0 + ROW_TILE], halo_prev, halo_next, g, wup_ref, cw_ref, cb_ref, wdn_ref,
                  gf_ref, out_ref.at[r0:r0 + ROW_TILE], act_ref.at[sub], perm_ref.at[sub])


def _mlp_rows(x_ref, halo_prev, halo_next, g, wup_ref, cw_ref, cb_ref, wdn_ref, gf_ref, out_ref,
              act_ref, perm_ref):
    t = ROW_TILE
    nv = t // HALO
    ncb = D_MODEL // PAIR
    x = x_ref[...]
    h = _rms(x, g)
    for r in range(HALO):
        for c in range(ncb):
            perm_ref[c, pl.ds(r, nv, stride=HALO), :] = h[r * nv:(r + 1) * nv, c * PAIR:(c + 1) * PAIR]
    hp = jnp.concatenate([perm_ref[c] for c in range(ncb)], axis=1)
    he = jnp.concatenate([hp, halo_prev, halo_next], axis=0).astype(BF16)

    def conv(u, c0):
        n = u.shape[1]
        w = cw_ref[:, c0:c0 + n]
        sub = lax.broadcasted_iota(jnp.int32, (HALO, n), 0)
        main, halo_prev, halo_next = u[:t], u[t:t + HALO], u[t + HALO:]
        head = jnp.where(sub == 0, halo_prev[HALO - 1:], pltpu.roll(main[t - HALO:], 1, axis=0))
        tail = jnp.where(sub == HALO - 1, halo_next[:1], pltpu.roll(main[:HALO], HALO - 1, axis=0))
        prev = jnp.concatenate([head, main[:t - HALO]], axis=0)
        nxt = jnp.concatenate([main[HALO:], tail], axis=0)
        return prev * w[0:1] + main * w[1:2] + nxt * w[2:3] + cb_ref[:, c0:c0 + n]

    for c0 in range(0, D_FF, FF_CHUNK):
        c1 = min(c0 + FF_CHUNK, D_FF)
        ug = jnp.dot(he, wup_ref[:, c0:c1], preferred_element_type=F32)
        uv = jnp.dot(he, wup_ref[:, D_FF + c0:D_FF + c1], preferred_element_type=F32)
        act_ref[:, c0:c1] = (_gelu_tanh(conv(ug, c0)) * conv(uv, D_FF + c0)).astype(BF16)
    yp = jnp.dot(act_ref[...], wdn_ref[...], preferred_element_type=F32)
    for c in range(ncb):
        perm_ref[c] = yp[:, c * PAIR:(c + 1) * PAIR]
    y = x + jnp.concatenate(
        [jnp.concatenate([perm_ref[c, pl.ds(r, nv, stride=HALO), :] for c in range(ncb)], axis=1)
         for r in range(HALO)], axis=0)
    out_ref[...] = _rms(y, gf_ref[...])


def _mlp_call(x, s, g, wup, cw, cb, wdn, gf):
    n = x.shape[0]
    t = MLP_STEP_TILES * ROW_TILE
    hb = t // HALO
    nhb = n // HALO
    consts = (g, wup, cw, cb, wdn, gf)
    return pl.pallas_call(
        functools.partial(_mlp_kernel, steps_per_seq=s // t),
        grid=(n // t,),
        in_specs=[pl.BlockSpec((HALO, D_MODEL), lambda i: (jnp.maximum(i * hb - 1, 0), 0)),
                  pl.BlockSpec((t, D_MODEL), lambda i: (i, 0)),
                  pl.BlockSpec((HALO, D_MODEL), lambda i: (jnp.minimum((i + 1) * hb, nhb - 1), 0))]
        + [_const_spec(c.shape) for c in consts],
        out_specs=pl.BlockSpec((t, D_MODEL), lambda i: (i, 0)),
        out_shape=jax.ShapeDtypeStruct((n, D_MODEL), F32),
        scratch_shapes=[pltpu.VMEM((MLP_STEP_TILES, ROW_TILE, D_FF), BF16),
                        pltpu.VMEM((MLP_STEP_TILES, D_MODEL // PAIR, ROW_TILE, PAIR), F32)],
        compiler_params=_params(1),
        name="conv_mlp",
    )(x, x, x, *consts)


def _rel_bucket(rel):
    nb = N_BUCKETS // 2
    max_exact = nb // 2
    rel = np.asarray(rel, np.int32)
    ret = np.where(rel > 0, nb, 0)
    n = np.abs(rel)
    nf = np.maximum(n, 1).astype(np.float32)
    large = max_exact + (np.log(nf / np.float32(max_exact)) / np.float32(math.log(MAX_DISTANCE / max_exact))
                         * np.float32(nb - max_exact)).astype(np.int32)
    large = np.minimum(large, nb - 1)
    return (ret + np.where(n < max_exact, n, large)).astype(np.int32)


def _bias_tiles(rel_bias, head0, n_heads, dilation, win, side, shifts):
    offs = np.arange(-side, side + 1) * dilation
    line = rel_bias[_rel_bucket(offs)][:, head0:head0 + n_heads].T.astype(F32) * LOG2E
    k0 = QBLK + max(shifts) - side
    p = win + k0 + side + 1
    padded = jnp.pad(line, ((0, 0), (k0, p - k0 - (2 * side + 1))), constant_values=NEG)
    skew = jnp.tile(padded, (1, QBLK + 1))[:, :QBLK * (p - 1)].reshape(n_heads, QBLK, p - 1)
    return jnp.stack([skew[:, :, k0 - sh + side:k0 - sh + side + win] for sh in shifts], 0)


def _prep(w_in, b_gate, rel_bias, sink, w_a_out, w_b_out, w_o, w_up, conv_w, conv_b, w_down,
          g_attn, g_ffn, g_final):
    row = lambda v: v.reshape(1, -1).astype(F32)
    p = dict(
        w_in=w_in.astype(BF16), b_gate=row(b_gate),
        w_a_out=w_a_out.astype(BF16), w_b_out=w_b_out.astype(BF16), w_o=w_o.astype(BF16),
        w_up=w_up.astype(BF16), conv_w=conv_w.astype(F32), conv_b=row(conv_b),
        w_down=w_down.astype(BF16), g_attn=row(g_attn), g_ffn=row(g_ffn), g_final=row(g_final),
        rel_bias=rel_bias,
    )
    bb = _bias_tiles(rel_bias, A_HEADS, B_Q_HEADS, 1, B_WIN, B_RADIUS, (0, B_RADIUS, 2 * B_RADIUS))
    p["bias_b"] = _pair_lanes(bb.reshape(3, B_KV_HEADS, 2, 2, QBLK, B_WIN)).transpose(1, 0, 2, 3, 4)
    p["sink"] = sink.astype(F32) * LOG2E
    return p


def _pair_lanes(t):
    return jnp.concatenate([t[..., 0, :, :], t[..., 1, :, :]], axis=-1)


def _group_biases(rel_bias, s):
    out = []
    for gi, d in enumerate(DILATIONS):
        m = s // d
        win = min(A_WIN, m)
        shifts = (0,) if m == QBLK else (0, A_SIDE, 2 * A_SIDE)
        t = _bias_tiles(rel_bias, gi * A_HEADS_PER_GROUP, A_HEADS_PER_GROUP, d, win, A_SIDE, shifts)
        out.append(_pair_lanes(t.reshape(len(shifts), 2, 2, QBLK, win)))
    return out


def _trunk(x, p):
    b, s, _ = x.shape
    xf = x.reshape(b * s, D_MODEL)
    g1, g2, g3, qb = _qkv_call(xf, p["g_attn"], p["w_in"], b, s)
    a = _attn_a_call((g1, g2, g3), _group_biases(p["rel_bias"], s), b, s)
    bo = _attn_b_call(qb, p["bias_b"], p["sink"], b, s)
    x1 = _post_call(xf, a, bo, p["g_attn"], p["w_in"], p["b_gate"], p["w_a_out"], p["w_b_out"], p["w_o"])
    y = _mlp_call(x1, s, p["g_ffn"], p["w_up"], p["conv_w"], p["conv_b"], p["w_down"], p["g_final"])
    return y.reshape(b, s, D_MODEL)


def kernel(x_prompt, x_sample, g_attn, w_in, b_gate, rel_bias, sink, w_a_out, w_b_out, w_o, g_ffn,
           w_up, conv_w, conv_b, w_down, g_final):
    assert w_in.shape[0] == 1, "one layer: the final RMSNorm is fused into its MLP kernel"
    p = _prep(w_in[0], b_gate[0], rel_bias, sink[0], w_a_out[0], w_b_out[0], w_o[0], w_up[0],
              conv_w[0], conv_b[0], w_down[0], g_attn[0], g_ffn[0], g_final)
    return (_trunk(x_prompt, p), _trunk(x_sample, p))
```

```python
import functools
import math

import numpy as np
import jax
import jax.numpy as jnp
from jax import lax
from jax.experimental import pallas as pl
from jax.experimental.pallas import tpu as pltpu

F32 = jnp.float32
BF16 = jnp.bfloat16

D_MODEL = 1024
HEAD_DIM = 64
DIL_PATTERNS = ((128, 1), (512, 4), (2048, 16))
DILATIONS = tuple(d for _, d in DIL_PATTERNS)
A_HEADS_PER_GROUP = 4
A_GROUP_WIDTH = A_HEADS_PER_GROUP * HEAD_DIM
N_GROUPS = len(DIL_PATTERNS)
A_HEADS = N_GROUPS * A_HEADS_PER_GROUP
A_WIDTH = A_HEADS * HEAD_DIM
A_SIDE = 64
B_Q_HEADS = 8
B_KV_HEADS = 2
B_GROUP = B_Q_HEADS // B_KV_HEADS
B_Q_WIDTH = B_Q_HEADS * HEAD_DIM
B_KV_WIDTH = B_KV_HEADS * HEAD_DIM
B_RADIUS = 128
N_BUCKETS = 32
MAX_DISTANCE = 1024
D_FF = 2816
EPS = 1e-6
NEG = -1e30
LOG2E = math.log2(math.e)
Q_SCALE = HEAD_DIM ** -0.5 * LOG2E

KA_OFF, VA_OFF = A_WIDTH, 2 * A_WIDTH
QB_OFF = 3 * A_WIDTH
KVB_OFF = QB_OFF + B_Q_WIDTH
GATE_OFF = KVB_OFF + 2 * B_KV_WIDTH
QKV_COLS = GATE_OFF

PAIR = 2 * HEAD_DIM
PAIR_COLS = 3 * PAIR
GRP_COLS = 2 * PAIR_COLS
B_COLS = B_Q_WIDTH + 4 * PAIR
QBLK = 128
A_WIN = QBLK + 2 * A_SIDE
B_WIN = QBLK + 2 * B_RADIUS
ROW_TILE = 512
STEP_TILES = 2
MLP_STEP_TILES = 2
MXU_COLS = 256
FF_CHUNK = MXU_COLS
HALO = 8
VMEM_LIMIT = 56 * 1024 * 1024


def _params(n_axes):
    return pltpu.CompilerParams(dimension_semantics=("arbitrary",) * n_axes,
                                vmem_limit_bytes=VMEM_LIMIT)


def _const_spec(shape, index=None):
    index = (0,) * len(shape) if index is None else index
    return pl.BlockSpec(shape, lambda *_: index, pipeline_mode=pl.Buffered(1))


def _rms(x, g):
    ms = jnp.mean(x * x, axis=-1, keepdims=True)
    return x * lax.rsqrt(ms + EPS) * g


def _lane_lo(rows=QBLK):
    return lax.broadcasted_iota(jnp.int32, (rows, PAIR), 1) < HEAD_DIM


def _pair_rhs(v):
    lo = _lane_lo(v.shape[0])
    zero = jnp.zeros_like(v)
    ones_lo = jnp.where(lo, 1.0, 0.0).astype(v.dtype)
    ones_hi = jnp.where(lo, 0.0, 1.0).astype(v.dtype)
    top = jnp.concatenate([jnp.where(lo, v, zero), ones_lo], axis=1)
    bot = jnp.concatenate([jnp.where(lo, zero, v), ones_hi], axis=1)
    return jnp.concatenate([top, bot], axis=0)


def _pair_scores(q, k):
    lo = _lane_lo(k.shape[0])
    zero = jnp.zeros_like(k)
    kbd = jnp.concatenate([jnp.where(lo, k, zero), jnp.where(lo, zero, k)], axis=0)
    return lax.dot_general(q, kbd, (((1,), (1,)), ((), ())), preferred_element_type=F32)


def _pair_softmax(sc, floors=(None, None)):
    win = sc.shape[1] // 2
    out = []
    for half, floor in enumerate(floors):
        s = sc[:, half * win:(half + 1) * win]
        m = jnp.broadcast_to(jnp.max(s, axis=-1, keepdims=True), (sc.shape[0], PAIR))
        if floor is not None:
            m = jnp.maximum(m, floor)
        out.append((m, jnp.exp2(s - jnp.concatenate([m] * (win // PAIR), axis=1))))
    return out


def _qkv_kernel(x_ref, g_ref, w_ref, o1_ref, o2_ref, o3_ref, ob_ref, tmp_ref):
    for sub in range(x_ref.shape[0] // ROW_TILE):
        _qkv_rows(sub, x_ref, g_ref, w_ref, (o1_ref, o2_ref, o3_ref), ob_ref, tmp_ref.at[sub])


def _qkv_rows(sub, x_ref, g_ref, w_ref, o_refs, ob_ref, tmp_ref):
    t = ROW_TILE
    rows = slice(sub * t, (sub + 1) * t)
    h = _rms(x_ref[rows, :], g_ref[...]).astype(BF16)

    def proj(c0, n):
        return jnp.dot(h, w_ref[:, c0:c0 + n], preferred_element_type=F32)

    for gi, (o_ref, d) in enumerate(zip(o_refs, DILATIONS)):
        c0 = gi * A_GROUP_WIDTH
        parts = (proj(c0, A_GROUP_WIDTH) * Q_SCALE, proj(KA_OFF + c0, A_GROUP_WIDTH),
                 proj(VA_OFF + c0, A_GROUP_WIDTH))
        for p in range(2):
            for j, part in enumerate(parts):
                c = p * 3 + j
                val = part[:, p * PAIR:(p + 1) * PAIR]
                if d == 1:
                    o_ref[0, rows, c * PAIR:(c + 1) * PAIR] = val.astype(BF16)
                else:
                    tmp_ref[c] = val
        if d > 1:
            td = t // d
            for r in range(d):
                for c in range(GRP_COLS // PAIR):
                    o_ref[r, sub * td:(sub + 1) * td, c * PAIR:(c + 1) * PAIR] = (
                        tmp_ref[c, pl.ds(r, td, stride=d), :].astype(BF16))

    ob_ref[rows, :B_Q_WIDTH] = (proj(QB_OFF, B_Q_WIDTH) * Q_SCALE).astype(BF16)
    kv = proj(KVB_OFF, 2 * B_KV_WIDTH)
    lo = _lane_lo(t)
    for j in range(2):
        one = kv[:, j * PAIR:(j + 1) * PAIR]
        swapped = pltpu.roll(one, HEAD_DIM, axis=1)
        c0 = B_Q_WIDTH + j * 2 * PAIR
        ob_ref[rows, c0:c0 + PAIR] = jnp.where(lo, one, swapped).astype(BF16)
        ob_ref[rows, c0 + PAIR:c0 + 2 * PAIR] = jnp.where(lo, swapped, one).astype(BF16)


def _qkv_call(x, g, w_in, b, s):
    n = x.shape[0]
    t = STEP_TILES * ROW_TILE
    tps = s // t
    grp_shape = lambda d: jax.ShapeDtypeStruct((b, d, s // d, GRP_COLS), BF16)
    grp_spec = lambda d: pl.BlockSpec((None, d, t // d, GRP_COLS), lambda i: (i // tps, 0, i % tps, 0))
    return pl.pallas_call(
        _qkv_kernel,
        grid=(n // t,),
        in_specs=[pl.BlockSpec((t, D_MODEL), lambda i: (i, 0)), _const_spec(g.shape),
                  _const_spec((D_MODEL, QKV_COLS))],
        out_specs=[grp_spec(d) for d in DILATIONS] + [pl.BlockSpec((t, B_COLS), lambda i: (i, 0))],
        out_shape=[grp_shape(d) for d in DILATIONS] + [jax.ShapeDtypeStruct((n, B_COLS), BF16)],
        scratch_shapes=[pltpu.VMEM((STEP_TILES, GRP_COLS // PAIR, ROW_TILE, PAIR), F32)],
        compiler_params=_params(1),
        name="qkv_proj",
    )(x, g, w_in)


def _window(i, nblk, m, win, side):
    if nblk == 1:
        return 0, 0
    start = pl.multiple_of(jnp.clip(i * QBLK - side, 0, m - win), 64)
    var = jnp.where(i == 0, 0, jnp.where(i == nblk - 1, 2, 1))
    return start, var


def _attn_a_kernel(g1_ref, g2_ref, g3_ref, b1_ref, b2_ref, b3_ref, out_ref, acc_ref, m_ref, l_ref, *, s):
    lo = _lane_lo()

    def attend(q, k, v, bias):
        (m_e, e_e), (m_o, e_o) = _pair_softmax(_pair_scores(q, k) + bias)
        e = jnp.concatenate([e_e, e_o], axis=1).astype(BF16)
        r = jnp.dot(e, _pair_rhs(v), preferred_element_type=F32)
        return r[:, :PAIR], jnp.where(lo, m_e, m_o), r[:, PAIR:]

    order = sorted(range(N_GROUPS), key=lambda gi: -DILATIONS[gi])
    assert DILATIONS[order[-1]] == 1
    refs, bias_refs = (g1_ref, g2_ref, g3_ref), (b1_ref, b2_ref, b3_ref)
    for step, gi in enumerate(order):
        ref, bias_ref, d = refs[gi], bias_refs[gi], DILATIONS[gi]
        m = s // d
        nblk = m // QBLK
        win = min(A_WIN, m)
        shift = nblk.bit_length() - 1
        assert nblk == 1 << shift

        def body(j, carry, ref=ref, bias_ref=bias_ref, d=d, m=m, nblk=nblk, win=win, shift=shift, step=step):
            r, i = (0, j) if d == 1 else (j >> shift, j & (nblk - 1))
            q0 = pl.multiple_of(i * QBLK, QBLK)
            start, var = _window(i, nblk, m, win, A_SIDE)
            o, mm, ll = attend(ref[r, pl.ds(q0, QBLK), :PAIR], ref[r, pl.ds(start, win), PAIR:2 * PAIR],
                               ref[r, pl.ds(start, win), 2 * PAIR:], bias_ref[var])
            rows = pl.ds(q0, QBLK) if d == 1 else pl.ds(r + d * q0, QBLK, stride=d)
            if step == 0:
                acc_ref[rows, :] = o
                m_ref[rows, :] = mm
                l_ref[rows, :] = ll
                return carry
            m_old = m_ref[rows, :]
            m_new = jnp.maximum(m_old, mm)
            alpha = jnp.exp2(m_old - m_new)
            beta = jnp.exp2(mm - m_new)
            l_new = alpha * l_ref[rows, :] + beta * ll
            acc_new = alpha * acc_ref[rows, :] + beta * o
            if step == N_GROUPS - 1:
                out_ref[rows, :] = (acc_new / l_new).astype(BF16)
            else:
                m_ref[rows, :] = m_new
                l_ref[rows, :] = l_new
                acc_ref[rows, :] = acc_new
            return carry

        lax.fori_loop(0, d * nblk, body, 0, unroll=16)


def _attn_a_call(groups, biases, b, s):
    in_specs = [pl.BlockSpec((None, d, s // d, PAIR_COLS), lambda p, bi: (bi, 0, 0, p)) for d in DILATIONS]
    in_specs += [pl.BlockSpec((bias.shape[0], None) + bias.shape[2:], lambda p, bi: (0, p, 0, 0))
                 for bias in biases]
    out = pl.pallas_call(
        functools.partial(_attn_a_kernel, s=s),
        grid=(2, b),
        in_specs=in_specs,
        out_specs=pl.BlockSpec((None, s, PAIR), lambda p, bi: (bi, 0, p)),
        out_shape=jax.ShapeDtypeStruct((b, s, A_GROUP_WIDTH), BF16),
        scratch_shapes=[pltpu.VMEM((s, PAIR), F32)] * 3,
        compiler_params=_params(2),
        name="attn_dil",
    )(*groups, *biases)
    return out.reshape(b * s, A_GROUP_WIDTH)


def _attn_b_kernel(q_ref, k_ref, v_ref, bias_ref, sink_ref, o_ref, *, s):
    nblk = s // QBLK
    lo = _lane_lo()
    c = pl.program_id(0)

    def block(i, carry):
        q0 = pl.multiple_of(i * QBLK, QBLK)
        start, var = _window(i, nblk, s, B_WIN, B_RADIUS)
        q = jnp.concatenate([q_ref[pl.ds(q0, QBLK), :PAIR], q_ref[pl.ds(q0, QBLK), PAIR:]], axis=0)
        sc = _pair_scores(q, k_ref[pl.ds(start, B_WIN), :])
        rhs = _pair_rhs(v_ref[pl.ds(start, B_WIN), :])
        es, sinks = [], []
        for p in range(2):
            sk_e = sink_ref[c * B_GROUP + 2 * p]
            sk_o = sink_ref[c * B_GROUP + 2 * p + 1]
            (m_e, e_e), (m_o, e_o) = _pair_softmax(sc[p * QBLK:(p + 1) * QBLK] + bias_ref[var, p], (sk_e, sk_o))
            es.append(jnp.concatenate([e_e, e_o], axis=1).astype(BF16))
            sinks.append(jnp.where(lo, jnp.exp2(sk_e - m_e), jnp.exp2(sk_o - m_o)))
        r = jnp.dot(jnp.concatenate(es, axis=0), rhs, preferred_element_type=F32)
        for p in range(2):
            rp = r[p * QBLK:(p + 1) * QBLK]
            o_ref[pl.ds(q0, QBLK), p * PAIR:(p + 1) * PAIR] = (rp[:, :PAIR] / (rp[:, PAIR:] + sinks[p])).astype(BF16)
        return carry

    lax.fori_loop(0, nblk, block, 0, unroll=16)


def _attn_b_call(qkv_b, bias, sink, b, s):
    qkv_v = qkv_b.reshape(b, s, B_COLS)
    qw = B_GROUP * HEAD_DIM
    kv0 = B_Q_WIDTH // PAIR
    o = pl.pallas_call(
        functools.partial(_attn_b_kernel, s=s),
        grid=(B_KV_HEADS, b),
        in_specs=[pl.BlockSpec((None, s, qw), lambda c, bi: (bi, 0, c)),
                  pl.BlockSpec((None, s, PAIR), lambda c, bi: (bi, 0, kv0 + c)),
                  pl.BlockSpec((None, s, PAIR), lambda c, bi: (bi, 0, kv0 + B_KV_HEADS + c)),
                  pl.BlockSpec((None,) + bias.shape[1:], lambda c, bi: (c, 0, 0, 0, 0)),
                  pl.BlockSpec(memory_space=pltpu.SMEM)],
        out_specs=pl.BlockSpec((None, s, qw), lambda c, bi: (bi, 0, c)),
        out_shape=jax.ShapeDtypeStruct((b, s, B_Q_WIDTH), BF16),
        compiler_params=_params(2),
        name="attn_win",
    )(qkv_v, qkv_v, qkv_v, bias, sink)
    return o.reshape(b * s, B_Q_WIDTH)


def _post_kernel(x_ref, a_ref, bo_ref, g_ref, wga_ref, wgb_ref, bg_ref, wa_ref, wb_ref, wo_ref,
                 out_ref, mix_ref):
    for sub in range(x_ref.shape[0] // ROW_TILE):
        rows = slice(sub * ROW_TILE, (sub + 1) * ROW_TILE)
        x = x_ref[rows, :]
        h = _rms(x, g_ref[...]).astype(BF16)
        a = a_ref[rows, :]
        bo = bo_ref[rows, :]
        for c0 in range(0, D_MODEL, MXU_COLS):
            cs = slice(c0, c0 + MXU_COLS)
            ap = jnp.dot(a, wa_ref[:, cs], preferred_element_type=F32)
            bp = jnp.dot(bo, wb_ref[:, cs], preferred_element_type=F32)
            ga = jnp.dot(h, wga_ref[:, cs], preferred_element_type=F32) + bg_ref[:, cs]
            gb = (jnp.dot(h, wgb_ref[:, cs], preferred_element_type=F32)
                  + bg_ref[:, D_MODEL + c0:D_MODEL + c0 + MXU_COLS])
            mix_ref[rows, cs] = (jax.nn.sigmoid(ga) * ap + jax.nn.sigmoid(gb) * bp).astype(BF16)
        out_ref[rows, :] = x + jnp.dot(mix_ref[rows, :], wo_ref[...], preferred_element_type=F32)


def _post_call(x, a, bo, g, w_in, bg, wa, wb, wo):
    n = x.shape[0]
    t = STEP_TILES * ROW_TILE
    rows = lambda w: pl.BlockSpec((t, w), lambda i: (i, 0))
    gate_blk = GATE_OFF // D_MODEL
    return pl.pallas_call(
        _post_kernel,
        grid=(n // t,),
        in_specs=[rows(D_MODEL), rows(A_GROUP_WIDTH), rows(B_Q_WIDTH), _const_spec(g.shape),
                  _const_spec((D_MODEL, D_MODEL), (0, gate_blk)),
                  _const_spec((D_MODEL, D_MODEL), (0, gate_blk + 1)),
                  _const_spec(bg.shape), _const_spec(wa.shape), _const_spec(wb.shape), _const_spec(wo.shape)],
        out_specs=rows(D_MODEL),
        out_shape=jax.ShapeDtypeStruct((n, D_MODEL), F32),
        scratch_shapes=[pltpu.VMEM((t, D_MODEL), BF16)],
        compiler_params=_params(1),
        name="post_attn",
    )(x, a, bo, g, w_in, w_in, bg, wa, wb, wo)


def _gelu_tanh(x):
    k = -2.0 * LOG2E * math.sqrt(2.0 / math.pi)
    return x / (1.0 + jnp.exp2(x * (k + (k * 0.044715) * (x * x))))


def _mlp_kernel(xp_ref, x_ref, xn_ref, g_ref, wup_ref, cw_ref, cb_ref, wdn_ref, gf_ref, out_ref,
                act_ref, perm_ref, *, steps_per_seq):
    n_sub = x_ref.shape[0] // ROW_TILE
    i = pl.program_id(0)
    keep_prev = jnp.where((i % steps_per_seq) == 0, 0.0, 1.0)
    keep_next = jnp.where((i % steps_per_seq) == steps_per_seq - 1, 0.0, 1.0)
    g = g_ref[...]
    for sub in range(n_sub):
        r0 = sub * ROW_TILE
        halo_prev = (_rms(xp_ref[...], g) * keep_prev if sub == 0
                     else _rms(x_ref[r0 - HALO:r0, :], g))
        halo_next = (_rms(xn_ref[...], g) * keep_next if sub == n_sub - 1
                     else _rms(x_ref[r0 + ROW_TILE:r0 + ROW_TILE + HALO, :], g))
        _mlp_rows(x_ref.at[r0:r0 + ROW_TILE], halo_prev, halo_next, g, wup_ref, cw_ref, cb_ref, wdn_ref,
                  gf_ref, out_ref.at[r0:r0 + ROW_TILE], act_ref.at[sub], perm_ref.at[sub])


def _mlp_rows(x_ref, halo_prev, halo_next, g, wup_ref, cw_ref, cb_ref, wdn_ref, gf_ref, out_ref,
              act_ref, perm_ref):
    t = ROW_TILE
    nv = t // HALO
    ncb = D_MODEL // PAIR
    x = x_ref[...]
    h = _rms(x, g)
    for r in range(HALO):
        for c in range(ncb):
            perm_ref[c, pl.ds(r, nv, stride=HALO), :] = h[r * nv:(r + 1) * nv, c * PAIR:(c + 1) * PAIR]
    hp = jnp.concatenate([perm_ref[c] for c in range(ncb)], axis=1)
    sub_d = lax.broadcasted_iota(jnp.int32, (HALO, D_MODEL), 0)
    halo = jnp.where(sub_d == 0, halo_prev[HALO - 1:], jnp.where(sub_d == 1, halo_next[:1], 0.0))
    he = jnp.concatenate([hp, halo], axis=0).astype(BF16)

    def conv(u, c0):
        n = u.shape[1]
        w = cw_ref[:, c0:c0 + n]
        sub = lax.broadcasted_iota(jnp.int32, (HALO, n), 0)
        main, u_halo = u[:t], u[t:]
        head = jnp.where(sub == 0, u_halo[0:1], pltpu.roll(main[t - HALO:], 1, axis=0))
        tail = jnp.where(sub == HALO - 1, u_halo[1:2], pltpu.roll(main[:HALO], HALO - 1, axis=0))
        prev = jnp.concatenate([head, main[:t - HALO]], axis=0)
        nxt = jnp.concatenate([main[HALO:], tail], axis=0)
        return prev * w[0:1] + main * w[1:2] + nxt * w[2:3] + cb_ref[:, c0:c0 + n]

    for c0 in range(0, D_FF, FF_CHUNK):
        c1 = min(c0 + FF_CHUNK, D_FF)
        ug = jnp.dot(he, wup_ref[:, c0:c1], preferred_element_type=F32)
        uv = jnp.dot(he, wup_ref[:, D_FF + c0:D_FF + c1], preferred_element_type=F32)
        act_ref[:, c0:c1] = (_gelu_tanh(conv(ug, c0)) * conv(uv, D_FF + c0)).astype(BF16)
    yp = jnp.dot(act_ref[...], wdn_ref[...], preferred_element_type=F32)
    for c in range(ncb):
        perm_ref[c] = yp[:, c * PAIR:(c + 1) * PAIR]
    y = x + jnp.concatenate(
        [jnp.concatenate([perm_ref[c, pl.ds(r, nv, stride=HALO), :] for c in range(ncb)], axis=1)
         for r in range(HALO)], axis=0)
    out_ref[...] = _rms(y, gf_ref[...])


def _mlp_call(x, s, g, wup, cw, cb, wdn, gf):
    n = x.shape[0]
    t = MLP_STEP_TILES * ROW_TILE
    hb = t // HALO
    nhb = n // HALO
    consts = (g, wup, cw, cb, wdn, gf)
    return pl.pallas_call(
        functools.partial(_mlp_kernel, steps_per_seq=s // t),
        grid=(n // t,),
        in_specs=[pl.BlockSpec((HALO, D_MODEL), lambda i: (jnp.maximum(i * hb - 1, 0), 0)),
                  pl.BlockSpec((t, D_MODEL), lambda i: (i, 0)),
                  pl.BlockSpec((HALO, D_MODEL), lambda i: (jnp.minimum((i + 1) * hb, nhb - 1), 0))]
        + [_const_spec(c.shape) for c in consts],
        out_specs=pl.BlockSpec((t, D_MODEL), lambda i: (i, 0)),
        out_shape=jax.ShapeDtypeStruct((n, D_MODEL), F32),
        scratch_shapes=[pltpu.VMEM((MLP_STEP_TILES, ROW_TILE, D_FF), BF16),
                        pltpu.VMEM((MLP_STEP_TILES, D_MODEL // PAIR, ROW_TILE, PAIR), F32)],
        compiler_params=_params(1),
        name="conv_mlp",
    )(x, x, x, *consts)


def _rel_bucket(rel):
    nb = N_BUCKETS // 2
    max_exact = nb // 2
    rel = np.asarray(rel, np.int32)
    ret = np.where(rel > 0, nb, 0)
    n = np.abs(rel)
    nf = np.maximum(n, 1).astype(np.float32)
    large = max_exact + (np.log(nf / np.float32(max_exact)) / np.float32(math.log(MAX_DISTANCE / max_exact))
                         * np.float32(nb - max_exact)).astype(np.int32)
    large = np.minimum(large, nb - 1)
    return (ret + np.where(n < max_exact, n, large)).astype(np.int32)


def _bias_tiles(rel_bias, head0, n_heads, dilation, win, side, shifts):
    offs = np.arange(-side, side + 1) * dilation
    line = rel_bias[_rel_bucket(offs)][:, head0:head0 + n_heads].T.astype(F32) * LOG2E
    k0 = QBLK + max(shifts) - side
    p = win + k0 + side + 1
    padded = jnp.pad(line, ((0, 0), (k0, p - k0 - (2 * side + 1))), constant_values=NEG)
    skew = jnp.tile(padded, (1, QBLK + 1))[:, :QBLK * (p - 1)].reshape(n_heads, QBLK, p - 1)
    return jnp.stack([skew[:, :, k0 - sh + side:k0 - sh + side + win] for sh in shifts], 0)


def _prep(w_in, b_gate, rel_bias, sink, w_a_out, w_b_out, w_o, w_up, conv_w, conv_b, w_down,
          g_attn, g_ffn, g_final):
    row = lambda v: v.reshape(1, -1).astype(F32)
    p = dict(
        w_in=w_in.astype(BF16), b_gate=row(b_gate),
        w_a_out=w_a_out.astype(BF16), w_b_out=w_b_out.astype(BF16), w_o=w_o.astype(BF16),
        w_up=w_up.astype(BF16), conv_w=conv_w.astype(F32), conv_b=row(conv_b),
        w_down=w_down.astype(BF16), g_attn=row(g_attn), g_ffn=row(g_ffn), g_final=row(g_final),
        rel_bias=rel_bias,
    )
    bb = _bias_tiles(rel_bias, A_HEADS, B_Q_HEADS, 1, B_WIN, B_RADIUS, (0, B_RADIUS, 2 * B_RADIUS))
    p["bias_b"] = _pair_lanes(bb.reshape(3, B_KV_HEADS, 2, 2, QBLK, B_WIN)).transpose(1, 0, 2, 3, 4)
    p["sink"] = sink.astype(F32) * LOG2E
    return p


def _pair_lanes(t):
    return jnp.concatenate([t[..., 0, :, :], t[..., 1, :, :]], axis=-1)


def _group_biases(rel_bias, s):
    out = []
    for gi, d in enumerate(DILATIONS):
        m = s // d
        win = min(A_WIN, m)
        shifts = (0,) if m == QBLK else (0, A_SIDE, 2 * A_SIDE)
        t = _bias_tiles(rel_bias, gi * A_HEADS_PER_GROUP, A_HEADS_PER_GROUP, d, win, A_SIDE, shifts)
        out.append(_pair_lanes(t.reshape(len(shifts), 2, 2, QBLK, win)))
    return out


def _trunk(x, p):
    b, s, _ = x.shape
    xf = x.reshape(b * s, D_MODEL)
    g1, g2, g3, qb = _qkv_call(xf, p["g_attn"], p["w_in"], b, s)
    a = _attn_a_call((g1, g2, g3), _group_biases(p["rel_bias"], s), b, s)
    bo = _attn_b_call(qb, p["bias_b"], p["sink"], b, s)
    x1 = _post_call(xf, a, bo, p["g_attn"], p["w_in"], p["b_gate"], p["w_a_out"], p["w_b_out"], p["w_o"])
    y = _mlp_call(x1, s, p["g_ffn"], p["w_up"], p["conv_w"], p["conv_b"], p["w_down"], p["g_final"])
    return y.reshape(b, s, D_MODEL)


def kernel(x_prompt, x_sample, g_attn, w_in, b_gate, rel_bias, sink, w_a_out, w_b_out, w_o, g_ffn,
           w_up, conv_w, conv_b, w_down, g_final):
    assert w_in.shape[0] == 1, "one layer: the final RMSNorm is fused into its MLP kernel"
    p = _prep(w_in[0], b_gate[0], rel_bias, sink[0], w_a_out[0], w_b_out[0], w_o[0], w_up[0],
              conv_w[0], conv_b[0], w_down[0], g_attn[0], g_ffn[0], g_final)
    return (_trunk(x_prompt, p), _trunk(x_sample, p))
```

```python
import functools
import math

import numpy as np
import jax
import jax.numpy as jnp
from jax import lax
from jax.experimental import pallas as pl
from jax.experimental.pallas import tpu as pltpu

F32 = jnp.float32
BF16 = jnp.bfloat16

D_MODEL = 1024
HEAD_DIM = 64
DIL_PATTERNS = ((128, 1), (512, 4), (2048, 16))
DILATIONS = tuple(d for _, d in DIL_PATTERNS)
A_HEADS_PER_GROUP = 4
A_GROUP_WIDTH = A_HEADS_PER_GROUP * HEAD_DIM
N_GROUPS = len(DIL_PATTERNS)
A_HEADS = N_GROUPS * A_HEADS_PER_GROUP
A_WIDTH = A_HEADS * HEAD_DIM
A_SIDE = 64
B_Q_HEADS = 8
B_KV_HEADS = 2
B_GROUP = B_Q_HEADS // B_KV_HEADS
B_Q_WIDTH = B_Q_HEADS * HEAD_DIM
B_KV_WIDTH = B_KV_HEADS * HEAD_DIM
B_RADIUS = 128
N_BUCKETS = 32
MAX_DISTANCE = 1024
D_FF = 2816
EPS = 1e-6
NEG = -1e30
LOG2E = math.log2(math.e)
Q_SCALE = HEAD_DIM ** -0.5 * LOG2E

KA_OFF, VA_OFF = A_WIDTH, 2 * A_WIDTH
QB_OFF = 3 * A_WIDTH
KVB_OFF = QB_OFF + B_Q_WIDTH
GATE_OFF = KVB_OFF + 2 * B_KV_WIDTH
QKV_COLS = GATE_OFF

PAIR = 2 * HEAD_DIM
PAIR_COLS = 3 * PAIR
GRP_COLS = 2 * PAIR_COLS
B_COLS = B_Q_WIDTH + 4 * PAIR
QBLK = 128
A_WIN = QBLK + 2 * A_SIDE
B_WIN = QBLK + 2 * B_RADIUS
ROW_TILE = 512
STEP_TILES = 2
MLP_STEP_TILES = 2
MXU_COLS = 256
FF_CHUNK = MXU_COLS
HALO = 8
VMEM_LIMIT = 56 * 1024 * 1024


def _params(n_axes):
    return pltpu.CompilerParams(dimension_semantics=("arbitrary",) * n_axes,
                                vmem_limit_bytes=VMEM_LIMIT)


def _const_spec(shape, index=None):
    index = (0,) * len(shape) if index is None else index
    return pl.BlockSpec(shape, lambda *_: index, pipeline_mode=pl.Buffered(1))


def _rms(x, g):
    ms = jnp.mean(x * x, axis=-1, keepdims=True)
    return x * lax.rsqrt(ms + EPS) * g


def _lane_lo(rows=QBLK):
    return lax.broadcasted_iota(jnp.int32, (rows, PAIR), 1) < HEAD_DIM


def _pair_rhs(v):
    lo = _lane_lo(v.shape[0])
    zero = jnp.zeros_like(v)
    ones_lo = jnp.where(lo, 1.0, 0.0).astype(v.dtype)
    ones_hi = jnp.where(lo, 0.0, 1.0).astype(v.dtype)
    top = jnp.concatenate([jnp.where(lo, v, zero), ones_lo], axis=1)
    bot = jnp.concatenate([jnp.where(lo, zero, v), ones_hi], axis=1)
    return jnp.concatenate([top, bot], axis=0)


def _pair_scores(q, k):
    lo = _lane_lo(k.shape[0])
    zero = jnp.zeros_like(k)
    kbd = jnp.concatenate([jnp.where(lo, k, zero), jnp.where(lo, zero, k)], axis=0)
    return lax.dot_general(q, kbd, (((1,), (1,)), ((), ())), preferred_element_type=F32)


def _pair_softmax(sc, floors=(None, None)):
    win = sc.shape[1] // 2
    out = []
    for half, floor in enumerate(floors):
        s = sc[:, half * win:(half + 1) * win]
        m = jnp.broadcast_to(jnp.max(s, axis=-1, keepdims=True), (sc.shape[0], PAIR))
        if floor is not None:
            m = jnp.maximum(m, floor)
        out.append((m, jnp.exp2(s - jnp.concatenate([m] * (win // PAIR), axis=1))))
    return out


def _qkv_kernel(x_ref, g_ref, w_ref, o1_ref, o2_ref, o3_ref, ob_ref, tmp_ref):
    for sub in range(x_ref.shape[0] // ROW_TILE):
        _qkv_rows(sub, x_ref, g_ref, w_ref, (o1_ref, o2_ref, o3_ref), ob_ref, tmp_ref.at[sub])


def _qkv_rows(sub, x_ref, g_ref, w_ref, o_refs, ob_ref, tmp_ref):
    t = ROW_TILE
    rows = slice(sub * t, (sub + 1) * t)
    h = _rms(x_ref[rows, :], g_ref[...]).astype(BF16)

    def proj(c0, n):
        return jnp.dot(h, w_ref[:, c0:c0 + n], preferred_element_type=F32)

    for gi, (o_ref, d) in enumerate(zip(o_refs, DILATIONS)):
        c0 = gi * A_GROUP_WIDTH
        parts = (proj(c0, A_GROUP_WIDTH) * Q_SCALE, proj(KA_OFF + c0, A_GROUP_WIDTH),
                 proj(VA_OFF + c0, A_GROUP_WIDTH))
        for p in range(2):
            for j, part in enumerate(parts):
                c = p * 3 + j
                val = part[:, p * PAIR:(p + 1) * PAIR]
                if d == 1:
                    o_ref[0, rows, c * PAIR:(c + 1) * PAIR] = val.astype(BF16)
                else:
                    tmp_ref[c] = val
        if d > 1:
            td = t // d
            for r in range(d):
                for c in range(GRP_COLS // PAIR):
                    o_ref[r, sub * td:(sub + 1) * td, c * PAIR:(c + 1) * PAIR] = (
                        tmp_ref[c, pl.ds(r, td, stride=d), :].astype(BF16))

    ob_ref[rows, :B_Q_WIDTH] = (proj(QB_OFF, B_Q_WIDTH) * Q_SCALE).astype(BF16)
    kv = proj(KVB_OFF, 2 * B_KV_WIDTH)
    lo = _lane_lo(t)
    for j in range(2):
        one = kv[:, j * PAIR:(j + 1) * PAIR]
        swapped = pltpu.roll(one, HEAD_DIM, axis=1)
        c0 = B_Q_WIDTH + j * 2 * PAIR
        ob_ref[rows, c0:c0 + PAIR] = jnp.where(lo, one, swapped).astype(BF16)
        ob_ref[rows, c0 + PAIR:c0 + 2 * PAIR] = jnp.where(lo, swapped, one).astype(BF16)


def _qkv_call(x, g, w_in, b, s):
    n = x.shape[0]
    t = STEP_TILES * ROW_TILE
    tps = s // t
    grp_shape = lambda d: jax.ShapeDtypeStruct((b, d, s // d, GRP_COLS), BF16)
    grp_spec = lambda d: pl.BlockSpec((None, d, t // d, GRP_COLS), lambda i: (i // tps, 0, i % tps, 0))
    return pl.pallas_call(
        _qkv_kernel,
        grid=(n // t,),
        in_specs=[pl.BlockSpec((t, D_MODEL), lambda i: (i, 0)), _const_spec(g.shape),
                  _const_spec((D_MODEL, QKV_COLS))],
        out_specs=[grp_spec(d) for d in DILATIONS] + [pl.BlockSpec((t, B_COLS), lambda i: (i, 0))],
        out_shape=[grp_shape(d) for d in DILATIONS] + [jax.ShapeDtypeStruct((n, B_COLS), BF16)],
        scratch_shapes=[pltpu.VMEM((STEP_TILES, GRP_COLS // PAIR, ROW_TILE, PAIR), F32)],
        compiler_params=_params(1),
        name="qkv_proj",
    )(x, g, w_in)


def _window(i, nblk, m, win, side):
    if nblk == 1:
        return 0, 0
    start = pl.multiple_of(jnp.clip(i * QBLK - side, 0, m - win), 64)
    var = jnp.where(i == 0, 0, jnp.where(i == nblk - 1, 2, 1))
    return start, var


def _attn_a_kernel(g1_ref, g2_ref, g3_ref, b1_ref, b2_ref, b3_ref, out_ref, part_ref, *, s):
    lo = _lane_lo()

    def attend(q, k, v, bias):
        (m_e, e_e), (m_o, e_o) = _pair_softmax(_pair_scores(q, k) + bias)
        e = jnp.concatenate([e_e, e_o], axis=1).astype(BF16)
        r = jnp.dot(e, _pair_rhs(v), preferred_element_type=F32)
        return r[:, :PAIR], jnp.where(lo, m_e, m_o), r[:, PAIR:]

    order = sorted(range(N_GROUPS), key=lambda gi: -DILATIONS[gi])
    assert DILATIONS[order[-1]] == 1
    refs, bias_refs = (g1_ref, g2_ref, g3_ref), (b1_ref, b2_ref, b3_ref)
    for step, gi in enumerate(order):
        ref, bias_ref, d = refs[gi], bias_refs[gi], DILATIONS[gi]
        m = s // d
        nblk = m // QBLK
        win = min(A_WIN, m)
        shift = nblk.bit_length() - 1
        assert nblk == 1 << shift

        def body(j, carry, ref=ref, bias_ref=bias_ref, d=d, m=m, nblk=nblk, win=win, shift=shift, step=step):
            r, i = (0, j) if d == 1 else (j >> shift, j & (nblk - 1))
            q0 = pl.multiple_of(i * QBLK, QBLK)
            start, var = _window(i, nblk, m, win, A_SIDE)
            o, mm, ll = attend(ref[r, pl.ds(q0, QBLK), :PAIR], ref[r, pl.ds(start, win), PAIR:2 * PAIR],
                               ref[r, pl.ds(start, win), 2 * PAIR:], bias_ref[var])
            rows = pl.ds(q0, QBLK) if d == 1 else pl.ds(r + d * q0, QBLK, stride=d)
            if step < N_GROUPS - 1:
                part_ref[step, 0, rows, :] = o
                part_ref[step, 1, rows, :] = mm
                part_ref[step, 2, rows, :] = ll
                return carry
            parts = [(o, mm, ll)] + [tuple(part_ref[k, j, rows, :] for j in range(3))
                                     for k in range(N_GROUPS - 1)]
            m_all = functools.reduce(jnp.maximum, [pm for _, pm, _ in parts])
            acc = 0.0
            den = 0.0
            for po, pm, pl_ in parts:
                w = jnp.exp2(pm - m_all)
                acc = acc + w * po
                den = den + w * pl_
            out_ref[rows, :] = (acc / den).astype(BF16)
            return carry

        lax.fori_loop(0, d * nblk, body, 0, unroll=16)


def _attn_a_call(groups, biases, b, s):
    in_specs = [pl.BlockSpec((None, d, s // d, PAIR_COLS), lambda p, bi: (bi, 0, 0, p)) for d in DILATIONS]
    in_specs += [pl.BlockSpec((bias.shape[0], None) + bias.shape[2:], lambda p, bi: (0, p, 0, 0))
                 for bias in biases]
    out = pl.pallas_call(
        functools.partial(_attn_a_kernel, s=s),
        grid=(2, b),
        in_specs=in_specs,
        out_specs=pl.BlockSpec((None, s, PAIR), lambda p, bi: (bi, 0, p)),
        out_shape=jax.ShapeDtypeStruct((b, s, A_GROUP_WIDTH), BF16),
        scratch_shapes=[pltpu.VMEM((N_GROUPS - 1, 3, s, PAIR), F32)],
        compiler_params=_params(2),
        name="attn_dil",
    )(*groups, *biases)
    return out.reshape(b * s, A_GROUP_WIDTH)


def _attn_b_kernel(q_ref, k_ref, v_ref, bias_ref, sink_ref, o_ref, *, s):
    nblk = s // QBLK
    lo = _lane_lo()
    c = pl.program_id(0)

    def block(i, carry):
        q0 = pl.multiple_of(i * QBLK, QBLK)
        start, var = _window(i, nblk, s, B_WIN, B_RADIUS)
        q = jnp.concatenate([q_ref[pl.ds(q0, QBLK), :PAIR], q_ref[pl.ds(q0, QBLK), PAIR:]], axis=0)
        sc = _pair_scores(q, k_ref[pl.ds(start, B_WIN), :])
        rhs = _pair_rhs(v_ref[pl.ds(start, B_WIN), :])
        es, sinks = [], []
        for p in range(2):
            sk_e = sink_ref[c * B_GROUP + 2 * p]
            sk_o = sink_ref[c * B_GROUP + 2 * p + 1]
            (m_e, e_e), (m_o, e_o) = _pair_softmax(sc[p * QBLK:(p + 1) * QBLK] + bias_ref[var, p], (sk_e, sk_o))
            es.append(jnp.concatenate([e_e, e_o], axis=1).astype(BF16))
            sinks.append(jnp.where(lo, jnp.exp2(sk_e - m_e), jnp.exp2(sk_o - m_o)))
        r = jnp.dot(jnp.concatenate(es, axis=0), rhs, preferred_element_type=F32)
        for p in range(2):
            rp = r[p * QBLK:(p + 1) * QBLK]
            o_ref[pl.ds(q0, QBLK), p * PAIR:(p + 1) * PAIR] = (rp[:, :PAIR] / (rp[:, PAIR:] + sinks[p])).astype(BF16)
        return carry

    lax.fori_loop(0, nblk, block, 0, unroll=16)


def _attn_b_call(qkv_b, bias, sink, b, s):
    qkv_v = qkv_b.reshape(b, s, B_COLS)
    qw = B_GROUP * HEAD_DIM
    kv0 = B_Q_WIDTH // PAIR
    o = pl.pallas_call(
        functools.partial(_attn_b_kernel, s=s),
        grid=(B_KV_HEADS, b),
        in_specs=[pl.BlockSpec((None, s, qw), lambda c, bi: (bi, 0, c)),
                  pl.BlockSpec((None, s, PAIR), lambda c, bi: (bi, 0, kv0 + c)),
                  pl.BlockSpec((None, s, PAIR), lambda c, bi: (bi, 0, kv0 + B_KV_HEADS + c)),
                  pl.BlockSpec((None,) + bias.shape[1:], lambda c, bi: (c, 0, 0, 0, 0)),
                  pl.BlockSpec(memory_space=pltpu.SMEM)],
        out_specs=pl.BlockSpec((None, s, qw), lambda c, bi: (bi, 0, c)),
        out_shape=jax.ShapeDtypeStruct((b, s, B_Q_WIDTH), BF16),
        compiler_params=_params(2),
        name="attn_win",
    )(qkv_v, qkv_v, qkv_v, bias, sink)
    return o.reshape(b * s, B_Q_WIDTH)


def _post_kernel(x_ref, a_ref, bo_ref, g_ref, wga_ref, wgb_ref, bg_ref, wa_ref, wb_ref, wo_ref,
                 out_ref, mix_ref):
    for sub in range(x_ref.shape[0] // ROW_TILE):
        rows = slice(sub * ROW_TILE, (sub + 1) * ROW_TILE)
        x = x_ref[rows, :]
        h = _rms(x, g_ref[...]).astype(BF16)
        a = a_ref[rows, :]
        bo = bo_ref[rows, :]
        for c0 in range(0, D_MODEL, MXU_COLS):
            cs = slice(c0, c0 + MXU_COLS)
            ap = jnp.dot(a, wa_ref[:, cs], preferred_element_type=F32)
            bp = jnp.dot(bo, wb_ref[:, cs], preferred_element_type=F32)
            ga = jnp.dot(h, wga_ref[:, cs], preferred_element_type=F32) + bg_ref[:, cs]
            gb = (jnp.dot(h, wgb_ref[:, cs], preferred_element_type=F32)
                  + bg_ref[:, D_MODEL + c0:D_MODEL + c0 + MXU_COLS])
            mix_ref[rows, cs] = (jax.nn.sigmoid(ga) * ap + jax.nn.sigmoid(gb) * bp).astype(BF16)
        out_ref[rows, :] = x + jnp.dot(mix_ref[rows, :], wo_ref[...], preferred_element_type=F32)


def _post_call(x, a, bo, g, w_in, bg, wa, wb, wo):
    n = x.shape[0]
    t = STEP_TILES * ROW_TILE
    rows = lambda w: pl.BlockSpec((t, w), lambda i: (i, 0))
    gate_blk = GATE_OFF // D_MODEL
    return pl.pallas_call(
        _post_kernel,
        grid=(n // t,),
        in_specs=[rows(D_MODEL), rows(A_GROUP_WIDTH), rows(B_Q_WIDTH), _const_spec(g.shape),
                  _const_spec((D_MODEL, D_MODEL), (0, gate_blk)),
                  _const_spec((D_MODEL, D_MODEL), (0, gate_blk + 1)),
                  _const_spec(bg.shape), _const_spec(wa.shape), _const_spec(wb.shape), _const_spec(wo.shape)],
        out_specs=rows(D_MODEL),
        out_shape=jax.ShapeDtypeStruct((n, D_MODEL), F32),
        scratch_shapes=[pltpu.VMEM((t, D_MODEL), BF16)],
        compiler_params=_params(1),
        name="post_attn",
    )(x, a, bo, g, w_in, w_in, bg, wa, wb, wo)


def _gelu_tanh(x):
    k = -2.0 * LOG2E * math.sqrt(2.0 / math.pi)
    return x / (1.0 + jnp.exp2(x * (k + (k * 0.044715) * (x * x))))


def _mlp_kernel(xp_ref, x_ref, xn_ref, g_ref, wup_ref, cw_ref, cb_ref, wdn_ref, gf_ref, out_ref,
                act_ref, perm_ref, *, steps_per_seq):
    n_sub = x_ref.shape[0] // ROW_TILE
    i = pl.program_id(0)
    keep_prev = jnp.where((i % steps_per_seq) == 0, 0.0, 1.0)
    keep_next = jnp.where((i % steps_per_seq) == steps_per_seq - 1, 0.0, 1.0)
    g = g_ref[...]
    for sub in range(n_sub):
        r0 = sub * ROW_TILE
        halo_prev = (_rms(xp_ref[...], g) * keep_prev if sub == 0
                     else _rms(x_ref[r0 - HALO:r0, :], g))
        halo_next = (_rms(xn_ref[...], g) * keep_next if sub == n_sub - 1
                     else _rms(x_ref[r0 + ROW_TILE:r0 + ROW_TILE + HALO, :], g))
        _mlp_rows(x_ref.at[r0:r0 + ROW_TILE], halo_prev, halo_next, g, wup_ref, cw_ref, cb_ref, wdn_ref,
                  gf_ref, out_ref.at[r0:r0 + ROW_TILE], act_ref.at[sub], perm_ref.at[sub])


def _mlp_rows(x_ref, halo_prev, halo_next, g, wup_ref, cw_ref, cb_ref, wdn_ref, gf_ref, out_ref,
              act_ref, perm_ref):
    t = ROW_TILE
    nv = t // HALO
    ncb = D_MODEL // PAIR
    x = x_ref[...]
    h = _rms(x, g)
    for r in range(HALO):
        for c in range(ncb):
            perm_ref[c, pl.ds(r, nv, stride=HALO), :] = h[r * nv:(r + 1) * nv, c * PAIR:(c + 1) * PAIR]
    hp = jnp.concatenate([perm_ref[c] for c in range(ncb)], axis=1)
    sub_d = lax.broadcasted_iota(jnp.int32, (HALO, D_MODEL), 0)
    halo = jnp.where(sub_d == 0, halo_prev[HALO - 1:], jnp.where(sub_d == 1, halo_next[:1], 0.0))
    he = jnp.concatenate([hp, halo], axis=0).astype(BF16)

    def conv(u, c0):
        n = u.shape[1]
        w = cw_ref[:, c0:c0 + n]
        sub = lax.broadcasted_iota(jnp.int32, (HALO, n), 0)
        main, u_halo = u[:t], u[t:]
        head = jnp.where(sub == 0, u_halo[0:1], pltpu.roll(main[t - HALO:], 1, axis=0))
        tail = jnp.where(sub == HALO - 1, u_halo[1:2], pltpu.roll(main[:HALO], HALO - 1, axis=0))
        prev = jnp.concatenate([head, main[:t - HALO]], axis=0)
        nxt = jnp.concatenate([main[HALO:], tail], axis=0)
        return prev * w[0:1] + main * w[1:2] + nxt * w[2:3] + cb_ref[:, c0:c0 + n]

    for c0 in range(0, D_FF, FF_CHUNK):
        c1 = min(c0 + FF_CHUNK, D_FF)
        ug = jnp.dot(he, wup_ref[:, c0:c1], preferred_element_type=F32)
        uv = jnp.dot(he, wup_ref[:, D_FF + c0:D_FF + c1], preferred_element_type=F32)
        act_ref[:, c0:c1] = (_gelu_tanh(conv(ug, c0)) * conv(uv, D_FF + c0)).astype(BF16)
    yp = jnp.dot(act_ref[...], wdn_ref[...], preferred_element_type=F32)
    for c in range(ncb):
        perm_ref[c] = yp[:, c * PAIR:(c + 1) * PAIR]
    y = x + jnp.concatenate(
        [jnp.concatenate([perm_ref[c, pl.ds(r, nv, stride=HALO), :] for c in range(ncb)], axis=1)
         for r in range(HALO)], axis=0)
    out_ref[...] = _rms(y, gf_ref[...])


def _mlp_call(x, s, g, wup, cw, cb, wdn, gf):
    n = x.shape[0]
    t = MLP_STEP_TILES * ROW_TILE
    hb = t // HALO
    nhb = n // HALO
    consts = (g, wup, cw, cb, wdn, gf)
    return pl.pallas_call(
        functools.partial(_mlp_kernel, steps_per_seq=s // t),
        grid=(n // t,),
        in_specs=[pl.BlockSpec((HALO, D_MODEL), lambda i: (jnp.maximum(i * hb - 1, 0), 0)),
                  pl.BlockSpec((t, D_MODEL), lambda i: (i, 0)),
                  pl.BlockSpec((HALO, D_MODEL), lambda i: (jnp.minimum((i + 1) * hb, nhb - 1), 0))]
        + [_const_spec(c.shape) for c in consts],
        out_specs=pl.BlockSpec((t, D_MODEL), lambda i: (i, 0)),
        out_shape=jax.ShapeDtypeStruct((n, D_MODEL), F32),
        scratch_shapes=[pltpu.VMEM((MLP_STEP_TILES, ROW_TILE, D_FF), BF16),
                        pltpu.VMEM((MLP_STEP_TILES, D_MODEL // PAIR, ROW_TILE, PAIR), F32)],
        compiler_params=_params(1),
        name="conv_mlp",
    )(x, x, x, *consts)


def _rel_bucket(rel):
    nb = N_BUCKETS // 2
    max_exact = nb // 2
    rel = np.asarray(rel, np.int32)
    ret = np.where(rel > 0, nb, 0)
    n = np.abs(rel)
    nf = np.maximum(n, 1).astype(np.float32)
    large = max_exact + (np.log(nf / np.float32(max_exact)) / np.float32(math.log(MAX_DISTANCE / max_exact))
                         * np.float32(nb - max_exact)).astype(np.int32)
    large = np.minimum(large, nb - 1)
    return (ret + np.where(n < max_exact, n, large)).astype(np.int32)


def _bias_tiles(rel_bias, head0, n_heads, dilation, win, side, shifts):
    offs = np.arange(-side, side + 1) * dilation
    line = rel_bias[_rel_bucket(offs)][:, head0:head0 + n_heads].T.astype(F32) * LOG2E
    k0 = QBLK + max(shifts) - side
    p = win + k0 + side + 1
    padded = jnp.pad(line, ((0, 0), (k0, p - k0 - (2 * side + 1))), constant_values=NEG)
    skew = jnp.tile(padded, (1, QBLK + 1))[:, :QBLK * (p - 1)].reshape(n_heads, QBLK, p - 1)
    return jnp.stack([skew[:, :, k0 - sh + side:k0 - sh + side + win] for sh in shifts], 0)


def _prep(w_in, b_gate, rel_bias, sink, w_a_out, w_b_out, w_o, w_up, conv_w, conv_b, w_down,
          g_attn, g_ffn, g_final):
    row = lambda v: v.reshape(1, -1).astype(F32)
    p = dict(
        w_in=w_in.astype(BF16), b_gate=row(b_gate),
        w_a_out=w_a_out.astype(BF16), w_b_out=w_b_out.astype(BF16), w_o=w_o.astype(BF16),
        w_up=w_up.astype(BF16), conv_w=conv_w.astype(F32), conv_b=row(conv_b),
        w_down=w_down.astype(BF16), g_attn=row(g_attn), g_ffn=row(g_ffn), g_final=row(g_final),
        rel_bias=rel_bias,
    )
    bb = _bias_tiles(rel_bias, A_HEADS, B_Q_HEADS, 1, B_WIN, B_RADIUS, (0, B_RADIUS, 2 * B_RADIUS))
    p["bias_b"] = _pair_lanes(bb.reshape(3, B_KV_HEADS, 2, 2, QBLK, B_WIN)).transpose(1, 0, 2, 3, 4)
    p["sink"] = sink.astype(F32) * LOG2E
    return p


def _pair_lanes(t):
    return jnp.concatenate([t[..., 0, :, :], t[..., 1, :, :]], axis=-1)


def _group_biases(rel_bias, s):
    out = []
    for gi, d in enumerate(DILATIONS):
        m = s // d
        win = min(A_WIN, m)
        shifts = (0,) if m == QBLK else (0, A_SIDE, 2 * A_SIDE)
        t = _bias_tiles(rel_bias, gi * A_HEADS_PER_GROUP, A_HEADS_PER_GROUP, d, win, A_SIDE, shifts)
        out.append(_pair_lanes(t.reshape(len(shifts), 2, 2, QBLK, win)))
    return out


def _trunk(x, p):
    b, s, _ = x.shape
    xf = x.reshape(b * s, D_MODEL)
    g1, g2, g3, qb = _qkv_call(xf, p["g_attn"], p["w_in"], b, s)
    a = _attn_a_call((g1, g2, g3), _group_biases(p["rel_bias"], s), b, s)
    bo = _attn_b_call(qb, p["bias_b"], p["sink"], b, s)
    x1 = _post_call(xf, a, bo, p["g_attn"], p["w_in"], p["b_gate"], p["w_a_out"], p["w_b_out"], p["w_o"])
    y = _mlp_call(x1, s, p["g_ffn"], p["w_up"], p["conv_w"], p["conv_b"], p["w_down"], p["g_final"])
    return y.reshape(b, s, D_MODEL)


def kernel(x_prompt, x_sample, g_attn, w_in, b_gate, rel_bias, sink, w_a_out, w_b_out, w_o, g_ffn,
           w_up, conv_w, conv_b, w_down, g_final):
    assert w_in.shape[0] == 1, "one layer: the final RMSNorm is fused into its MLP kernel"
    p = _prep(w_in[0], b_gate[0], rel_bias, sink[0], w_a_out[0], w_b_out[0], w_o[0], w_up[0],
              conv_w[0], conv_b[0], w_down[0], g_attn[0], g_ffn[0], g_final)
    return (_trunk(x_prompt, p), _trunk(x_sample, p))
```

```python
import functools
import math

import numpy as np
import jax
import jax.numpy as jnp
from jax import lax
from jax.experimental import pallas as pl
from jax.experimental.pallas import tpu as pltpu

F32 = jnp.float32
BF16 = jnp.bfloat16

D_MODEL = 1024
HEAD_DIM = 64
DIL_PATTERNS = ((128, 1), (512, 4), (2048, 16))
DILATIONS = tuple(d for _, d in DIL_PATTERNS)
A_HEADS_PER_GROUP = 4
A_GROUP_WIDTH = A_HEADS_PER_GROUP * HEAD_DIM
N_GROUPS = len(DIL_PATTERNS)
A_HEADS = N_GROUPS * A_HEADS_PER_GROUP
A_WIDTH = A_HEADS * HEAD_DIM
A_SIDE = 64
B_Q_HEADS = 8
B_KV_HEADS = 2
B_GROUP = B_Q_HEADS // B_KV_HEADS
B_Q_WIDTH = B_Q_HEADS * HEAD_DIM
B_KV_WIDTH = B_KV_HEADS * HEAD_DIM
B_RADIUS = 128
N_BUCKETS = 32
MAX_DISTANCE = 1024
D_FF = 2816
EPS = 1e-6
NEG = -1e30
LOG2E = math.log2(math.e)
Q_SCALE = HEAD_DIM ** -0.5 * LOG2E

KA_OFF, VA_OFF = A_WIDTH, 2 * A_WIDTH
QB_OFF = 3 * A_WIDTH
KVB_OFF = QB_OFF + B_Q_WIDTH
GATE_OFF = KVB_OFF + 2 * B_KV_WIDTH
QKV_COLS = GATE_OFF

PAIR = 2 * HEAD_DIM
PAIR_COLS = 3 * PAIR
GRP_COLS = 2 * PAIR_COLS
B_COLS = B_Q_WIDTH + 4 * PAIR
QBLK = 128
A_WIN = QBLK + 2 * A_SIDE
B_WIN = QBLK + 2 * B_RADIUS
ROW_TILE = 512
STEP_TILES = 2
MLP_STEP_TILES = 2
MXU_COLS = 256
FF_CHUNK = MXU_COLS
HALO = 8
SPLIT_FREE_STRIDE = 4
VMEM_LIMIT = 56 * 1024 * 1024


def _params(n_axes):
    return pltpu.CompilerParams(dimension_semantics=("arbitrary",) * n_axes,
                                vmem_limit_bytes=VMEM_LIMIT)


def _const_spec(shape, index=None):
    index = (0,) * len(shape) if index is None else index
    return pl.BlockSpec(shape, lambda *_: index, pipeline_mode=pl.Buffered(1))


def _rms(x, g):
    ms = jnp.mean(x * x, axis=-1, keepdims=True)
    return x * lax.rsqrt(ms + EPS) * g


def _lane_lo(rows=QBLK):
    return lax.broadcasted_iota(jnp.int32, (rows, PAIR), 1) < HEAD_DIM


def _pair_rhs(v):
    lo = _lane_lo(v.shape[0])
    zero = jnp.zeros_like(v)
    ones_lo = jnp.where(lo, 1.0, 0.0).astype(v.dtype)
    ones_hi = jnp.where(lo, 0.0, 1.0).astype(v.dtype)
    top = jnp.concatenate([jnp.where(lo, v, zero), ones_lo], axis=1)
    bot = jnp.concatenate([jnp.where(lo, zero, v), ones_hi], axis=1)
    return jnp.concatenate([top, bot], axis=0)


def _pair_scores(q, k):
    lo = _lane_lo(k.shape[0])
    zero = jnp.zeros_like(k)
    kbd = jnp.concatenate([jnp.where(lo, k, zero), jnp.where(lo, zero, k)], axis=0)
    return lax.dot_general(q, kbd, (((1,), (1,)), ((), ())), preferred_element_type=F32)


def _pair_softmax(sc, floors=(None, None)):
    win = sc.shape[1] // 2
    out = []
    for half, floor in enumerate(floors):
        s = sc[:, half * win:(half + 1) * win]
        m = jnp.broadcast_to(jnp.max(s, axis=-1, keepdims=True), (sc.shape[0], PAIR))
        if floor is not None:
            m = jnp.maximum(m, floor)
        out.append((m, jnp.exp2(s - jnp.concatenate([m] * (win // PAIR), axis=1))))
    return out


def _qkv_kernel(x_ref, g_ref, w_ref, o1_ref, o2_ref, o3_ref, ob_ref, tmp_ref, tmp2_ref):
    for sub in range(x_ref.shape[0] // ROW_TILE):
        _qkv_rows(sub, x_ref, g_ref, w_ref, (o1_ref, o2_ref, o3_ref), ob_ref, tmp_ref.at[sub],
                  tmp2_ref.at[sub])


def _qkv_rows(sub, x_ref, g_ref, w_ref, o_refs, ob_ref, tmp_ref, tmp2_ref):
    t = ROW_TILE
    rows = slice(sub * t, (sub + 1) * t)
    h = _rms(x_ref[rows, :], g_ref[...]).astype(BF16)

    def proj(c0, n):
        return jnp.dot(h, w_ref[:, c0:c0 + n], preferred_element_type=F32)

    for gi, (o_ref, d) in enumerate(zip(o_refs, DILATIONS)):
        c0 = gi * A_GROUP_WIDTH
        parts = (proj(c0, A_GROUP_WIDTH) * Q_SCALE, proj(KA_OFF + c0, A_GROUP_WIDTH),
                 proj(VA_OFF + c0, A_GROUP_WIDTH))
        for p in range(2):
            for j, part in enumerate(parts):
                c = p * 3 + j
                val = part[:, p * PAIR:(p + 1) * PAIR]
                if d == 1:
                    o_ref[0, rows, c * PAIR:(c + 1) * PAIR] = val.astype(BF16)
                else:
                    tmp_ref[c] = val
        if d > 1:
            td = t // d
            src_ref, d_in = tmp_ref, d
            if d > SPLIT_FREE_STRIDE:
                d_out = SPLIT_FREE_STRIDE
                d_in = d // d_out
                assert d_in <= SPLIT_FREE_STRIDE
                for r1 in range(d_out):
                    for c in range(GRP_COLS // PAIR):
                        tmp2_ref[c, r1 * (t // d_out):(r1 + 1) * (t // d_out), :] = (
                            tmp_ref[c, pl.ds(r1, t // d_out, stride=d_out), :])
                src_ref = tmp2_ref
            for r in range(d):
                base = (r % (d // d_in)) * (t // (d // d_in)) + r // (d // d_in)
                for c in range(GRP_COLS // PAIR):
                    o_ref[r, sub * td:(sub + 1) * td, c * PAIR:(c + 1) * PAIR] = (
                        src_ref[c, pl.ds(base, td, stride=d_in), :].astype(BF16))

    ob_ref[rows, :B_Q_WIDTH] = (proj(QB_OFF, B_Q_WIDTH) * Q_SCALE).astype(BF16)
    kv = proj(KVB_OFF, 2 * B_KV_WIDTH)
    lo = _lane_lo(t)
    for j in range(2):
        one = kv[:, j * PAIR:(j + 1) * PAIR]
        swapped = pltpu.roll(one, HEAD_DIM, axis=1)
        c0 = B_Q_WIDTH + j * 2 * PAIR
        ob_ref[rows, c0:c0 + PAIR] = jnp.where(lo, one, swapped).astype(BF16)
        ob_ref[rows, c0 + PAIR:c0 + 2 * PAIR] = jnp.where(lo, swapped, one).astype(BF16)


def _qkv_call(x, g, w_in, b, s):
    n = x.shape[0]
    t = STEP_TILES * ROW_TILE
    tps = s // t
    grp_shape = lambda d: jax.ShapeDtypeStruct((b, d, s // d, GRP_COLS), BF16)
    grp_spec = lambda d: pl.BlockSpec((None, d, t // d, GRP_COLS), lambda i: (i // tps, 0, i % tps, 0))
    return pl.pallas_call(
        _qkv_kernel,
        grid=(n // t,),
        in_specs=[pl.BlockSpec((t, D_MODEL), lambda i: (i, 0)), _const_spec(g.shape),
                  _const_spec((D_MODEL, QKV_COLS))],
        out_specs=[grp_spec(d) for d in DILATIONS] + [pl.BlockSpec((t, B_COLS), lambda i: (i, 0))],
        out_shape=[grp_shape(d) for d in DILATIONS] + [jax.ShapeDtypeStruct((n, B_COLS), BF16)],
        scratch_shapes=[pltpu.VMEM((STEP_TILES, GRP_COLS // PAIR, ROW_TILE, PAIR), F32)] * 2,
        compiler_params=_params(1),
        name="qkv_proj",
    )(x, g, w_in)


def _window(i, nblk, m, win, side):
    if nblk == 1:
        return 0, 0
    start = pl.multiple_of(jnp.clip(i * QBLK - side, 0, m - win), 64)
    var = jnp.where(i == 0, 0, jnp.where(i == nblk - 1, 2, 1))
    return start, var


def _attn_a_kernel(g1_ref, g2_ref, g3_ref, b1_ref, b2_ref, b3_ref, out_ref, acc_ref, m_ref, l_ref, *, s):
    lo = _lane_lo()

    def attend(q, k, v, bias):
        (m_e, e_e), (m_o, e_o) = _pair_softmax(_pair_scores(q, k) + bias)
        e = jnp.concatenate([e_e, e_o], axis=1).astype(BF16)
        r = jnp.dot(e, _pair_rhs(v), preferred_element_type=F32)
        return r[:, :PAIR], jnp.where(lo, m_e, m_o), r[:, PAIR:]

    order = sorted(range(N_GROUPS), key=lambda gi: -DILATIONS[gi])
    assert DILATIONS[order[-1]] == 1
    refs, bias_refs = (g1_ref, g2_ref, g3_ref), (b1_ref, b2_ref, b3_ref)
    for step, gi in enumerate(order):
        ref, bias_ref, d = refs[gi], bias_refs[gi], DILATIONS[gi]
        m = s // d
        nblk = m // QBLK
        win = min(A_WIN, m)
        shift = nblk.bit_length() - 1
        assert nblk == 1 << shift

        def body(j, carry, ref=ref, bias_ref=bias_ref, d=d, m=m, nblk=nblk, win=win, shift=shift, step=step):
            r, i = (0, j) if d == 1 else (j >> shift, j & (nblk - 1))
            q0 = pl.multiple_of(i * QBLK, QBLK)
            start, var = _window(i, nblk, m, win, A_SIDE)
            o, mm, ll = attend(ref[r, pl.ds(q0, QBLK), :PAIR], ref[r, pl.ds(start, win), PAIR:2 * PAIR],
                               ref[r, pl.ds(start, win), 2 * PAIR:], bias_ref[var])
            rows = pl.ds(q0, QBLK) if d == 1 else pl.ds(r + d * q0, QBLK, stride=d)
            if step == 0:
                acc_ref[rows, :] = o
                m_ref[rows, :] = mm
                l_ref[rows, :] = ll
                return carry
            m_old = m_ref[rows, :]
            m_new = jnp.maximum(m_old, mm)
            alpha = jnp.exp2(m_old - m_new)
            beta = jnp.exp2(mm - m_new)
            l_new = alpha * l_ref[rows, :] + beta * ll
            acc_new = alpha * acc_ref[rows, :] + beta * o
            if step == N_GROUPS - 1:
                out_ref[rows, :] = (acc_new / l_new).astype(BF16)
            else:
                m_ref[rows, :] = m_new
                l_ref[rows, :] = l_new
                acc_ref[rows, :] = acc_new
            return carry

        lax.fori_loop(0, d * nblk, body, 0, unroll=16)


def _attn_a_call(groups, biases, b, s):
    in_specs = [pl.BlockSpec((None, d, s // d, PAIR_COLS), lambda p, bi: (bi, 0, 0, p)) for d in DILATIONS]
    in_specs += [pl.BlockSpec((bias.shape[0], None) + bias.shape[2:], lambda p, bi: (0, p, 0, 0))
                 for bias in biases]
    out = pl.pallas_call(
        functools.partial(_attn_a_kernel, s=s),
        grid=(2, b),
        in_specs=in_specs,
        out_specs=pl.BlockSpec((None, s, PAIR), lambda p, bi: (bi, 0, p)),
        out_shape=jax.ShapeDtypeStruct((b, s, A_GROUP_WIDTH), BF16),
        scratch_shapes=[pltpu.VMEM((s, PAIR), F32)] * 3,
        compiler_params=_params(2),
        name="attn_dil",
    )(*groups, *biases)
    return out.reshape(b * s, A_GROUP_WIDTH)


def _attn_b_kernel(q_ref, k_ref, v_ref, bias_ref, sink_ref, o_ref, *, s):
    nblk = s // QBLK
    lo = _lane_lo()
    c = pl.program_id(0)

    def block(i, carry):
        q0 = pl.multiple_of(i * QBLK, QBLK)
        start, var = _window(i, nblk, s, B_WIN, B_RADIUS)
        q = jnp.concatenate([q_ref[pl.ds(q0, QBLK), :PAIR], q_ref[pl.ds(q0, QBLK), PAIR:]], axis=0)
        sc = _pair_scores(q, k_ref[pl.ds(start, B_WIN), :])
        rhs = _pair_rhs(v_ref[pl.ds(start, B_WIN), :])
        es, sinks = [], []
        for p in range(2):
            sk_e = sink_ref[c * B_GROUP + 2 * p]
            sk_o = sink_ref[c * B_GROUP + 2 * p + 1]
            (m_e, e_e), (m_o, e_o) = _pair_softmax(sc[p * QBLK:(p + 1) * QBLK] + bias_ref[var, p], (sk_e, sk_o))
            es.append(jnp.concatenate([e_e, e_o], axis=1).astype(BF16))
            sinks.append(jnp.where(lo, jnp.exp2(sk_e - m_e), jnp.exp2(sk_o - m_o)))
        r = jnp.dot(jnp.concatenate(es, axis=0), rhs, preferred_element_type=F32)
        for p in range(2):
            rp = r[p * QBLK:(p + 1) * QBLK]
            o_ref[pl.ds(q0, QBLK), p * PAIR:(p + 1) * PAIR] = (rp[:, :PAIR] / (rp[:, PAIR:] + sinks[p])).astype(BF16)
        return carry

    lax.fori_loop(0, nblk, block, 0, unroll=16)


def _attn_b_call(qkv_b, bias, sink, b, s):
    qkv_v = qkv_b.reshape(b, s, B_COLS)
    qw = B_GROUP * HEAD_DIM
    kv0 = B_Q_WIDTH // PAIR
    o = pl.pallas_call(
        functools.partial(_attn_b_kernel, s=s),
        grid=(B_KV_HEADS, b),
        in_specs=[pl.BlockSpec((None, s, qw), lambda c, bi: (bi, 0, c)),
                  pl.BlockSpec((None, s, PAIR), lambda c, bi: (bi, 0, kv0 + c)),
                  pl.BlockSpec((None, s, PAIR), lambda c, bi: (bi, 0, kv0 + B_KV_HEADS + c)),
                  pl.BlockSpec((None,) + bias.shape[1:], lambda c, bi: (c, 0, 0, 0, 0)),
                  pl.BlockSpec(memory_space=pltpu.SMEM)],
        out_specs=pl.BlockSpec((None, s, qw), lambda c, bi: (bi, 0, c)),
        out_shape=jax.ShapeDtypeStruct((b, s, B_Q_WIDTH), BF16),
        compiler_params=_params(2),
        name="attn_win",
    )(qkv_v, qkv_v, qkv_v, bias, sink)
    return o.reshape(b * s, B_Q_WIDTH)


def _post_kernel(x_ref, a_ref, bo_ref, g_ref, wga_ref, wgb_ref, bg_ref, wa_ref, wb_ref, wo_ref,
                 out_ref, mix_ref):
    for sub in range(x_ref.shape[0] // ROW_TILE):
        rows = slice(sub * ROW_TILE, (sub + 1) * ROW_TILE)
        x = x_ref[rows, :]
        h = _rms(x, g_ref[...]).astype(BF16)
        a = a_ref[rows, :]
        bo = bo_ref[rows, :]
        for c0 in range(0, D_MODEL, MXU_COLS):
            cs = slice(c0, c0 + MXU_COLS)
            ap = jnp.dot(a, wa_ref[:, cs], preferred_element_type=F32)
            bp = jnp.dot(bo, wb_ref[:, cs], preferred_element_type=F32)
            ga = jnp.dot(h, wga_ref[:, cs], preferred_element_type=F32) + bg_ref[:, cs]
            gb = (jnp.dot(h, wgb_ref[:, cs], preferred_element_type=F32)
                  + bg_ref[:, D_MODEL + c0:D_MODEL + c0 + MXU_COLS])
            mix_ref[rows, cs] = (jax.nn.sigmoid(ga) * ap + jax.nn.sigmoid(gb) * bp).astype(BF16)
        out_ref[rows, :] = x + jnp.dot(mix_ref[rows, :], wo_ref[...], preferred_element_type=F32)


def _post_call(x, a, bo, g, w_in, bg, wa, wb, wo):
    n = x.shape[0]
    t = STEP_TILES * ROW_TILE
    rows = lambda w: pl.BlockSpec((t, w), lambda i: (i, 0))
    gate_blk = GATE_OFF // D_MODEL
    return pl.pallas_call(
        _post_kernel,
        grid=(n // t,),
        in_specs=[rows(D_MODEL), rows(A_GROUP_WIDTH), rows(B_Q_WIDTH), _const_spec(g.shape),
                  _const_spec((D_MODEL, D_MODEL), (0, gate_blk)),
                  _const_spec((D_MODEL, D_MODEL), (0, gate_blk + 1)),
                  _const_spec(bg.shape), _const_spec(wa.shape), _const_spec(wb.shape), _const_spec(wo.shape)],
        out_specs=rows(D_MODEL),
        out_shape=jax.ShapeDtypeStruct((n, D_MODEL), F32),
        scratch_shapes=[pltpu.VMEM((t, D_MODEL), BF16)],
        compiler_params=_params(1),
        name="post_attn",
    )(x, a, bo, g, w_in, w_in, bg, wa, wb, wo)


def _gelu_tanh(x):
    k = -2.0 * LOG2E * math.sqrt(2.0 / math.pi)
    return x / (1.0 + jnp.exp2(x * (k + (k * 0.044715) * (x * x))))


def _mlp_kernel(xp_ref, x_ref, xn_ref, g_ref, wup_ref, cw_ref, cb_ref, wdn_ref, gf_ref, out_ref,
                act_ref, perm_ref, *, steps_per_seq):
    n_sub = x_ref.shape[0] // ROW_TILE
    i = pl.program_id(0)
    keep_prev = jnp.where((i % steps_per_seq) == 0, 0.0, 1.0)
    keep_next = jnp.where((i % steps_per_seq) == steps_per_seq - 1, 0.0, 1.0)
    g = g_ref[...]
    for sub in range(n_sub):
        r0 = sub * ROW_TILE
        halo_prev = (_rms(xp_ref[...], g) * keep_prev if sub == 0
                     else _rms(x_ref[r0 - HALO:r0, :], g))
        halo_next = (_rms(xn_ref[...], g) * keep_next if sub == n_sub - 1
                     else _rms(x_ref[r0 + ROW_TILE:r0 + ROW_TILE + HALO, :], g))
        _mlp_rows(x_ref.at[r0:r0 + ROW_TILE], halo_prev, halo_next, g, wup_ref, cw_ref, cb_ref, wdn_ref,
                  gf_ref, out_ref.at[r0:r0 + ROW_TILE], act_ref.at[sub], perm_ref.at[sub])


def _mlp_rows(x_ref, halo_prev, halo_next, g, wup_ref, cw_ref, cb_ref, wdn_ref, gf_ref, out_ref,
              act_ref, perm_ref):
    t = ROW_TILE
    nv = t // HALO
    ncb = D_MODEL // PAIR
    x = x_ref[...]
    h = _rms(x, g)
    for r in range(HALO):
        for c in range(ncb):
            perm_ref[c, pl.ds(r, nv, stride=HALO), :] = h[r * nv:(r + 1) * nv, c * PAIR:(c + 1) * PAIR]
    hp = jnp.concatenate([perm_ref[c] for c in range(ncb)], axis=1)
    sub_d = lax.broadcasted_iota(jnp.int32, (HALO, D_MODEL), 0)
    halo = jnp.where(sub_d == 0, halo_prev[HALO - 1:], jnp.where(sub_d == 1, halo_next[:1], 0.0))
    he = jnp.concatenate([hp, halo], axis=0).astype(BF16)

    def conv(u, c0):
        n = u.shape[1]
        w = cw_ref[:, c0:c0 + n]
        sub = lax.broadcasted_iota(jnp.int32, (HALO, n), 0)
        main, u_halo = u[:t], u[t:]
        head = jnp.where(sub == 0, u_halo[0:1], pltpu.roll(main[t - HALO:], 1, axis=0))
        tail = jnp.where(sub == HALO - 1, u_halo[1:2], pltpu.roll(main[:HALO], HALO - 1, axis=0))
        prev = jnp.concatenate([head, main[:t - HALO]], axis=0)
        nxt = jnp.concatenate([main[HALO:], tail], axis=0)
        return prev * w[0:1] + main * w[1:2] + nxt * w[2:3] + cb_ref[:, c0:c0 + n]

    for c0 in range(0, D_FF, FF_CHUNK):
        c1 = min(c0 + FF_CHUNK, D_FF)
        ug = jnp.dot(he, wup_ref[:, c0:c1], preferred_element_type=F32)
        uv = jnp.dot(he, wup_ref[:, D_FF + c0:D_FF + c1], preferred_element_type=F32)
        act_ref[:, c0:c1] = (_gelu_tanh(conv(ug, c0)) * conv(uv, D_FF + c0)).astype(BF16)
    yp = jnp.dot(act_ref[...], wdn_ref[...], preferred_element_type=F32)
    for c in range(ncb):
        perm_ref[c] = yp[:, c * PAIR:(c + 1) * PAIR]
    y = x + jnp.concatenate(
        [jnp.concatenate([perm_ref[c, pl.ds(r, nv, stride=HALO), :] for c in range(ncb)], axis=1)
         for r in range(HALO)], axis=0)
    out_ref[...] = _rms(y, gf_ref[...])


def _mlp_call(x, s, g, wup, cw, cb, wdn, gf):
    n = x.shape[0]
    t = MLP_STEP_TILES * ROW_TILE
    hb = t // HALO
    nhb = n // HALO
    consts = (g, wup, cw, cb, wdn, gf)
    return pl.pallas_call(
        functools.partial(_mlp_kernel, steps_per_seq=s // t),
        grid=(n // t,),
        in_specs=[pl.BlockSpec((HALO, D_MODEL), lambda i: (jnp.maximum(i * hb - 1, 0), 0)),
                  pl.BlockSpec((t, D_MODEL), lambda i: (i, 0)),
                  pl.BlockSpec((HALO, D_MODEL), lambda i: (jnp.minimum((i + 1) * hb, nhb - 1), 0))]
        + [_const_spec(c.shape) for c in consts],
        out_specs=pl.BlockSpec((t, D_MODEL), lambda i: (i, 0)),
        out_shape=jax.ShapeDtypeStruct((n, D_MODEL), F32),
        scratch_shapes=[pltpu.VMEM((MLP_STEP_TILES, ROW_TILE, D_FF), BF16),
                        pltpu.VMEM((MLP_STEP_TILES, D_MODEL // PAIR, ROW_TILE, PAIR), F32)],
        compiler_params=_params(1),
        name="conv_mlp",
    )(x, x, x, *consts)


def _rel_bucket(rel):
    nb = N_BUCKETS // 2
    max_exact = nb // 2
    rel = np.asarray(rel, np.int32)
    ret = np.where(rel > 0, nb, 0)
    n = np.abs(rel)
    nf = np.maximum(n, 1).astype(np.float32)
    large = max_exact + (np.log(nf / np.float32(max_exact)) / np.float32(math.log(MAX_DISTANCE / max_exact))
                         * np.float32(nb - max_exact)).astype(np.int32)
    large = np.minimum(large, nb - 1)
    return (ret + np.where(n < max_exact, n, large)).astype(np.int32)


def _bias_tiles(rel_bias, head0, n_heads, dilation, win, side, shifts):
    offs = np.arange(-side, side + 1) * dilation
    line = rel_bias[_rel_bucket(offs)][:, head0:head0 + n_heads].T.astype(F32) * LOG2E
    k0 = QBLK + max(shifts) - side
    p = win + k0 + side + 1
    padded = jnp.pad(line, ((0, 0), (k0, p - k0 - (2 * side + 1))), constant_values=NEG)
    skew = jnp.tile(padded, (1, QBLK + 1))[:, :QBLK * (p - 1)].reshape(n_heads, QBLK, p - 1)
    return jnp.stack([skew[:, :, k0 - sh + side:k0 - sh + side + win] for sh in shifts], 0)


def _prep(w_in, b_gate, rel_bias, sink, w_a_out, w_b_out, w_o, w_up, conv_w, conv_b, w_down,
          g_attn, g_ffn, g_final):
    row = lambda v: v.reshape(1, -1).astype(F32)
    p = dict(
        w_in=w_in.astype(BF16), b_gate=row(b_gate),
        w_a_out=w_a_out.astype(BF16), w_b_out=w_b_out.astype(BF16), w_o=w_o.astype(BF16),
        w_up=w_up.astype(BF16), conv_w=conv_w.astype(F32), conv_b=row(conv_b),
        w_down=w_down.astype(BF16), g_attn=row(g_attn), g_ffn=row(g_ffn), g_final=row(g_final),
        rel_bias=rel_bias,
    )
    bb = _bias_tiles(rel_bias, A_HEADS, B_Q_HEADS, 1, B_WIN, B_RADIUS, (0, B_RADIUS, 2 * B_RADIUS))
    p["bias_b"] = _pair_lanes(bb.reshape(3, B_KV_HEADS, 2, 2, QBLK, B_WIN)).transpose(1, 0, 2, 3, 4)
    p["sink"] = sink.astype(F32) * LOG2E
    return p


def _pair_lanes(t):
    return jnp.concatenate([t[..., 0, :, :], t[..., 1, :, :]], axis=-1)


def _group_biases(rel_bias, s):
    out = []
    for gi, d in enumerate(DILATIONS):
        m = s // d
        win = min(A_WIN, m)
        shifts = (0,) if m == QBLK else (0, A_SIDE, 2 * A_SIDE)
        t = _bias_tiles(rel_bias, gi * A_HEADS_PER_GROUP, A_HEADS_PER_GROUP, d, win, A_SIDE, shifts)
        out.append(_pair_lanes(t.reshape(len(shifts), 2, 2, QBLK, win)))
    return out


def _trunk(x, p):
    b, s, _ = x.shape
    xf = x.reshape(b * s, D_MODEL)
    g1, g2, g3, qb = _qkv_call(xf, p["g_attn"], p["w_in"], b, s)
    a = _attn_a_call((g1, g2, g3), _group_biases(p["rel_bias"], s), b, s)
    bo = _attn_b_call(qb, p["bias_b"], p["sink"], b, s)
    x1 = _post_call(xf, a, bo, p["g_attn"], p["w_in"], p["b_gate"], p["w_a_out"], p["w_b_out"], p["w_o"])
    y = _mlp_call(x1, s, p["g_ffn"], p["w_up"], p["conv_w"], p["conv_b"], p["w_down"], p["g_final"])
    return y.reshape(b, s, D_MODEL)


def kernel(x_prompt, x_sample, g_attn, w_in, b_gate, rel_bias, sink, w_a_out, w_b_out, w_o, g_ffn,
           w_up, conv_w, conv_b, w_down, g_final):
    assert w_in.shape[0] == 1, "one layer: the final RMSNorm is fused into its MLP kernel"
    p = _prep(w_in[0], b_gate[0], rel_bias, sink[0], w_a_out[0], w_b_out[0], w_o[0], w_up[0],
              conv_w[0], conv_b[0], w_down[0], g_attn[0], g_ffn[0], g_final)
    return (_trunk(x_prompt, p), _trunk(x_sample, p))
```

```python
import functools
import math

import numpy as np
import jax
import jax.numpy as jnp
from jax import lax
from jax.experimental import pallas as pl
from jax.experimental.pallas import tpu as pltpu

F32 = jnp.float32
BF16 = jnp.bfloat16

D_MODEL = 1024
HEAD_DIM = 64
DIL_PATTERNS = ((128, 1), (512, 4), (2048, 16))
DILATIONS = tuple(d for _, d in DIL_PATTERNS)
A_HEADS_PER_GROUP = 4
A_GROUP_WIDTH = A_HEADS_PER_GROUP * HEAD_DIM
N_GROUPS = len(DIL_PATTERNS)
A_HEADS = N_GROUPS * A_HEADS_PER_GROUP
A_WIDTH = A_HEADS * HEAD_DIM
A_SIDE = 64
B_Q_HEADS = 8
B_KV_HEADS = 2
B_GROUP = B_Q_HEADS // B_KV_HEADS
B_Q_WIDTH = B_Q_HEADS * HEAD_DIM
B_KV_WIDTH = B_KV_HEADS * HEAD_DIM
B_RADIUS = 128
N_BUCKETS = 32
MAX_DISTANCE = 1024
D_FF = 2816
EPS = 1e-6
NEG = -1e30
LOG2E = math.log2(math.e)
Q_SCALE = HEAD_DIM ** -0.5 * LOG2E

KA_OFF, VA_OFF = A_WIDTH, 2 * A_WIDTH
QB_OFF = 3 * A_WIDTH
KVB_OFF = QB_OFF + B_Q_WIDTH
GATE_OFF = KVB_OFF + 2 * B_KV_WIDTH
QKV_COLS = GATE_OFF

PAIR = 2 * HEAD_DIM
PAIR_COLS = 3 * PAIR
GRP_COLS = 2 * PAIR_COLS
B_COLS = B_Q_WIDTH + 4 * PAIR
QBLK = 128
A_WIN = QBLK + 2 * A_SIDE
B_WIN = QBLK + 2 * B_RADIUS
ROW_TILE = 512
STEP_TILES = 2
MLP_STEP_TILES = 2
MXU_COLS = 256
FF_CHUNK = MXU_COLS
HALO = 8
SPLIT_FREE_STRIDE = 4
VMEM_LIMIT = 56 * 1024 * 1024


def _params(n_axes):
    return pltpu.CompilerParams(dimension_semantics=("arbitrary",) * n_axes,
                                vmem_limit_bytes=VMEM_LIMIT)


def _const_spec(shape, index=None):
    index = (0,) * len(shape) if index is None else index
    return pl.BlockSpec(shape, lambda *_: index, pipeline_mode=pl.Buffered(1))


def _rms(x, g):
    ms = jnp.mean(x * x, axis=-1, keepdims=True)
    return x * lax.rsqrt(ms + EPS) * g


def _lane_lo(rows=QBLK):
    return lax.broadcasted_iota(jnp.int32, (rows, PAIR), 1) < HEAD_DIM


def _pair_rhs(v):
    lo = _lane_lo(v.shape[0])
    zero = jnp.zeros_like(v)
    ones_lo = jnp.where(lo, 1.0, 0.0).astype(v.dtype)
    ones_hi = jnp.where(lo, 0.0, 1.0).astype(v.dtype)
    top = jnp.concatenate([jnp.where(lo, v, zero), ones_lo], axis=1)
    bot = jnp.concatenate([jnp.where(lo, zero, v), ones_hi], axis=1)
    return jnp.concatenate([top, bot], axis=0)


def _pair_scores(q, k):
    lo = _lane_lo(k.shape[0])
    zero = jnp.zeros_like(k)
    kbd = jnp.concatenate([jnp.where(lo, k, zero), jnp.where(lo, zero, k)], axis=0)
    return lax.dot_general(q, kbd, (((1,), (1,)), ((), ())), preferred_element_type=F32)


def _pair_softmax(sc, floors=(None, None)):
    win = sc.shape[1] // 2
    out = []
    for half, floor in enumerate(floors):
        s = sc[:, half * win:(half + 1) * win]
        m = jnp.broadcast_to(jnp.max(s, axis=-1, keepdims=True), (sc.shape[0], PAIR))
        if floor is not None:
            m = jnp.maximum(m, floor)
        out.append((m, jnp.exp2(s - jnp.concatenate([m] * (win // PAIR), axis=1))))
    return out


def _qkv_kernel(x_ref, g_ref, w_ref, o1_ref, o2_ref, o3_ref, ob_ref, tmp_ref, tmp2_ref):
    for sub in range(x_ref.shape[0] // ROW_TILE):
        _qkv_rows(sub, x_ref, g_ref, w_ref, (o1_ref, o2_ref, o3_ref), ob_ref, tmp_ref.at[sub],
                  tmp2_ref.at[sub])


def _qkv_rows(sub, x_ref, g_ref, w_ref, o_refs, ob_ref, tmp_ref, tmp2_ref):
    t = ROW_TILE
    rows = slice(sub * t, (sub + 1) * t)
    h = _rms(x_ref[rows, :], g_ref[...]).astype(BF16)

    def proj(c0, n):
        return jnp.dot(h, w_ref[:, c0:c0 + n], preferred_element_type=F32)

    for gi, (o_ref, d) in enumerate(zip(o_refs, DILATIONS)):
        c0 = gi * A_GROUP_WIDTH
        parts = (proj(c0, A_GROUP_WIDTH) * Q_SCALE, proj(KA_OFF + c0, A_GROUP_WIDTH),
                 proj(VA_OFF + c0, A_GROUP_WIDTH))
        for p in range(2):
            for j, part in enumerate(parts):
                c = p * 3 + j
                val = part[:, p * PAIR:(p + 1) * PAIR]
                if d == 1:
                    o_ref[0, rows, c * PAIR:(c + 1) * PAIR] = val.astype(BF16)
                else:
                    tmp_ref[c] = val
        if d > 1:
            td = t // d
            src_ref, d_in = tmp_ref, d
            if d > SPLIT_FREE_STRIDE:
                d_out = SPLIT_FREE_STRIDE
                d_in = d // d_out
                assert d_in <= SPLIT_FREE_STRIDE
                for r1 in range(d_out):
                    for c in range(GRP_COLS // PAIR):
                        tmp2_ref[c, r1 * (t // d_out):(r1 + 1) * (t // d_out), :] = (
                            tmp_ref[c, pl.ds(r1, t // d_out, stride=d_out), :])
                src_ref = tmp2_ref
            for r in range(d):
                base = (r % (d // d_in)) * (t // (d // d_in)) + r // (d // d_in)
                for c in range(GRP_COLS // PAIR):
                    o_ref[r, sub * td:(sub + 1) * td, c * PAIR:(c + 1) * PAIR] = (
                        src_ref[c, pl.ds(base, td, stride=d_in), :].astype(BF16))

    ob_ref[rows, :B_Q_WIDTH] = (proj(QB_OFF, B_Q_WIDTH) * Q_SCALE).astype(BF16)
    kv = proj(KVB_OFF, 2 * B_KV_WIDTH)
    lo = _lane_lo(t)
    for j in range(2):
        one = kv[:, j * PAIR:(j + 1) * PAIR]
        swapped = pltpu.roll(one, HEAD_DIM, axis=1)
        c0 = B_Q_WIDTH + j * 2 * PAIR
        ob_ref[rows, c0:c0 + PAIR] = jnp.where(lo, one, swapped).astype(BF16)
        ob_ref[rows, c0 + PAIR:c0 + 2 * PAIR] = jnp.where(lo, swapped, one).astype(BF16)


def _qkv_call(x, g, w_in, b, s):
    n = x.shape[0]
    t = STEP_TILES * ROW_TILE
    tps = s // t
    grp_shape = lambda d: jax.ShapeDtypeStruct((b, d, s // d, GRP_COLS), BF16)
    grp_spec = lambda d: pl.BlockSpec((None, d, t // d, GRP_COLS), lambda i: (i // tps, 0, i % tps, 0))
    return pl.pallas_call(
        _qkv_kernel,
        grid=(n // t,),
        in_specs=[pl.BlockSpec((t, D_MODEL), lambda i: (i, 0)), _const_spec(g.shape),
                  _const_spec((D_MODEL, QKV_COLS))],
        out_specs=[grp_spec(d) for d in DILATIONS] + [pl.BlockSpec((t, B_COLS), lambda i: (i, 0))],
        out_shape=[grp_shape(d) for d in DILATIONS] + [jax.ShapeDtypeStruct((n, B_COLS), BF16)],
        scratch_shapes=[pltpu.VMEM((STEP_TILES, GRP_COLS // PAIR, ROW_TILE, PAIR), F32)] * 2,
        compiler_params=_params(1),
        name="qkv_proj",
    )(x, g, w_in)


def _window(i, nblk, m, win, side):
    if nblk == 1:
        return 0, 0
    start = pl.multiple_of(jnp.clip(i * QBLK - side, 0, m - win), 64)
    var = jnp.where(i == 0, 0, jnp.where(i == nblk - 1, 2, 1))
    return start, var


def _attn_a_kernel(g1_ref, g2_ref, g3_ref, b1_ref, b2_ref, b3_ref, out_ref, state_ref, *, s):
    lo = _lane_lo()

    def attend(q, k, v, bias):
        (m_e, e_e), (m_o, e_o) = _pair_softmax(_pair_scores(q, k) + bias)
        e = jnp.concatenate([e_e, e_o], axis=1).astype(BF16)
        r = jnp.dot(e, _pair_rhs(v), preferred_element_type=F32)
        return r[:, :PAIR], jnp.where(lo, m_e, m_o), r[:, PAIR:]

    order = sorted(range(N_GROUPS), key=lambda gi: -DILATIONS[gi])
    assert DILATIONS[order[-1]] == 1
    refs, bias_refs = (g1_ref, g2_ref, g3_ref), (b1_ref, b2_ref, b3_ref)
    for step, gi in enumerate(order):
        ref, bias_ref, d = refs[gi], bias_refs[gi], DILATIONS[gi]
        last = step == N_GROUPS - 1
        d_next = 1 if last else DILATIONS[order[step + 1]]
        assert d % d_next == 0 and d // d_next <= SPLIT_FREE_STRIDE
        src_ref, dst_ref = state_ref.at[(step + 1) % 2], state_ref.at[step % 2]
        m = s // d
        nblk = m // QBLK
        win = min(A_WIN, m)
        shift = nblk.bit_length() - 1
        assert nblk == 1 << shift

        def body(j, carry, ref=ref, bias_ref=bias_ref, d=d, m=m, nblk=nblk, win=win, shift=shift, step=step,
                 last=last, d_next=d_next, src_ref=src_ref, dst_ref=dst_ref):
            r, i = (0, j) if d == 1 else (j >> shift, j & (nblk - 1))
            q0 = pl.multiple_of(i * QBLK, QBLK)
            start, var = _window(i, nblk, m, win, A_SIDE)
            o, mm, ll = attend(ref[r, pl.ds(q0, QBLK), :PAIR], ref[r, pl.ds(start, win), PAIR:2 * PAIR],
                               ref[r, pl.ds(start, win), 2 * PAIR:], bias_ref[var])
            if step > 0:
                own = pl.ds(pl.multiple_of(r * m + q0, QBLK), QBLK)
                m_old = src_ref[1, own, :]
                m_new = jnp.maximum(m_old, mm)
                alpha = jnp.exp2(m_old - m_new)
                beta = jnp.exp2(mm - m_new)
                ll = alpha * src_ref[2, own, :] + beta * ll
                o = alpha * src_ref[0, own, :] + beta * o
                mm = m_new
            if last:
                out_ref[pl.ds(q0, QBLK), :] = (o / ll).astype(BF16)
                return carry
            rows = pl.ds((r % d_next) * (s // d_next) + r // d_next + (d // d_next) * q0, QBLK,
                         stride=d // d_next)
            dst_ref[0, rows, :] = o
            dst_ref[1, rows, :] = mm
            dst_ref[2, rows, :] = ll
            return carry

        lax.fori_loop(0, d * nblk, body, 0, unroll=16)


def _attn_a_call(groups, biases, b, s):
    in_specs = [pl.BlockSpec((None, d, s // d, PAIR_COLS), lambda p, bi: (bi, 0, 0, p)) for d in DILATIONS]
    in_specs += [pl.BlockSpec((bias.shape[0], None) + bias.shape[2:], lambda p, bi: (0, p, 0, 0))
                 for bias in biases]
    out = pl.pallas_call(
        functools.partial(_attn_a_kernel, s=s),
        grid=(2, b),
        in_specs=in_specs,
        out_specs=pl.BlockSpec((None, s, PAIR), lambda p, bi: (bi, 0, p)),
        out_shape=jax.ShapeDtypeStruct((b, s, A_GROUP_WIDTH), BF16),
        scratch_shapes=[pltpu.VMEM((2, 3, s, PAIR), F32)],
        compiler_params=_params(2),
        name="attn_dil",
    )(*groups, *biases)
    return out.reshape(b * s, A_GROUP_WIDTH)


def _attn_b_kernel(q_ref, k_ref, v_ref, bias_ref, sink_ref, o_ref, *, s):
    nblk = s // QBLK
    lo = _lane_lo()
    c = pl.program_id(0)

    def block(i, carry):
        q0 = pl.multiple_of(i * QBLK, QBLK)
        start, var = _window(i, nblk, s, B_WIN, B_RADIUS)
        q = jnp.concatenate([q_ref[pl.ds(q0, QBLK), :PAIR], q_ref[pl.ds(q0, QBLK), PAIR:]], axis=0)
        sc = _pair_scores(q, k_ref[pl.ds(start, B_WIN), :])
        rhs = _pair_rhs(v_ref[pl.ds(start, B_WIN), :])
        es, sinks = [], []
        for p in range(2):
            sk_e = sink_ref[c * B_GROUP + 2 * p]
            sk_o = sink_ref[c * B_GROUP + 2 * p + 1]
            (m_e, e_e), (m_o, e_o) = _pair_softmax(sc[p * QBLK:(p + 1) * QBLK] + bias_ref[var, p], (sk_e, sk_o))
            es.append(jnp.concatenate([e_e, e_o], axis=1).astype(BF16))
            sinks.append(jnp.where(lo, jnp.exp2(sk_e - m_e), jnp.exp2(sk_o - m_o)))
        r = jnp.dot(jnp.concatenate(es, axis=0), rhs, preferred_element_type=F32)
        for p in range(2):
            rp = r[p * QBLK:(p + 1) * QBLK]
            o_ref[pl.ds(q0, QBLK), p * PAIR:(p + 1) * PAIR] = (rp[:, :PAIR] / (rp[:, PAIR:] + sinks[p])).astype(BF16)
        return carry

    lax.fori_loop(0, nblk, block, 0, unroll=16)


def _attn_b_call(qkv_b, bias, sink, b, s):
    qkv_v = qkv_b.reshape(b, s, B_COLS)
    qw = B_GROUP * HEAD_DIM
    kv0 = B_Q_WIDTH // PAIR
    o = pl.pallas_call(
        functools.partial(_attn_b_kernel, s=s),
        grid=(B_KV_HEADS, b),
        in_specs=[pl.BlockSpec((None, s, qw), lambda c, bi: (bi, 0, c)),
                  pl.BlockSpec((None, s, PAIR), lambda c, bi: (bi, 0, kv0 + c)),
                  pl.BlockSpec((None, s, PAIR), lambda c, bi: (bi, 0, kv0 + B_KV_HEADS + c)),
                  pl.BlockSpec((None,) + bias.shape[1:], lambda c, bi: (c, 0, 0, 0, 0)),
                  pl.BlockSpec(memory_space=pltpu.SMEM)],
        out_specs=pl.BlockSpec((None, s, qw), lambda c, bi: (bi, 0, c)),
        out_shape=jax.ShapeDtypeStruct((b, s, B_Q_WIDTH), BF16),
        compiler_params=_params(2),
        name="attn_win",
    )(qkv_v, qkv_v, qkv_v, bias, sink)
    return o.reshape(b * s, B_Q_WIDTH)


def _post_kernel(x_ref, a_ref, bo_ref, g_ref, wga_ref, wgb_ref, bg_ref, wa_ref, wb_ref, wo_ref,
                 out_ref, mix_ref):
    for sub in range(x_ref.shape[0] // ROW_TILE):
        rows = slice(sub * ROW_TILE, (sub + 1) * ROW_TILE)
        x = x_ref[rows, :]
        h = _rms(x, g_ref[...]).astype(BF16)
        a = a_ref[rows, :]
        bo = bo_ref[rows, :]
        for c0 in range(0, D_MODEL, MXU_COLS):
            cs = slice(c0, c0 + MXU_COLS)
            ap = jnp.dot(a, wa_ref[:, cs], preferred_element_type=F32)
            bp = jnp.dot(bo, wb_ref[:, cs], preferred_element_type=F32)
            ga = jnp.dot(h, wga_ref[:, cs], preferred_element_type=F32) + bg_ref[:, cs]
            gb = (jnp.dot(h, wgb_ref[:, cs], preferred_element_type=F32)
                  + bg_ref[:, D_MODEL + c0:D_MODEL + c0 + MXU_COLS])
            mix_ref[rows, cs] = (jax.nn.sigmoid(ga) * ap + jax.nn.sigmoid(gb) * bp).astype(BF16)
        out_ref[rows, :] = x + jnp.dot(mix_ref[rows, :], wo_ref[...], preferred_element_type=F32)


def _post_call(x, a, bo, g, w_in, bg, wa, wb, wo):
    n = x.shape[0]
    t = STEP_TILES * ROW_TILE
    rows = lambda w: pl.BlockSpec((t, w), lambda i: (i, 0))
    gate_blk = GATE_OFF // D_MODEL
    return pl.pallas_call(
        _post_kernel,
        grid=(n // t,),
        in_specs=[rows(D_MODEL), rows(A_GROUP_WIDTH), rows(B_Q_WIDTH), _const_spec(g.shape),
                  _const_spec((D_MODEL, D_MODEL), (0, gate_blk)),
                  _const_spec((D_MODEL, D_MODEL), (0, gate_blk + 1)),
                  _const_spec(bg.shape), _const_spec(wa.shape), _const_spec(wb.shape), _const_spec(wo.shape)],
        out_specs=rows(D_MODEL),
        out_shape=jax.ShapeDtypeStruct((n, D_MODEL), F32),
        scratch_shapes=[pltpu.VMEM((t, D_MODEL), BF16)],
        compiler_params=_params(1),
        name="post_attn",
    )(x, a, bo, g, w_in, w_in, bg, wa, wb, wo)


def _gelu_tanh(x):
    k = -2.0 * LOG2E * math.sqrt(2.0 / math.pi)
    return x / (1.0 + jnp.exp2(x * (k + (k * 0.044715) * (x * x))))


def _mlp_kernel(xp_ref, x_ref, xn_ref, g_ref, wup_ref, cw_ref, cb_ref, wdn_ref, gf_ref, out_ref,
                act_ref, perm_ref, *, steps_per_seq):
    n_sub = x_ref.shape[0] // ROW_TILE
    i = pl.program_id(0)
    keep_prev = jnp.where((i % steps_per_seq) == 0, 0.0, 1.0)
    keep_next = jnp.where((i % steps_per_seq) == steps_per_seq - 1, 0.0, 1.0)
    g = g_ref[...]
    for sub in range(n_sub):
        r0 = sub * ROW_TILE
        halo_prev = (_rms(xp_ref[...], g) * keep_prev if sub == 0
                     else _rms(x_ref[r0 - HALO:r0, :], g))
        halo_next = (_rms(xn_ref[...], g) * keep_next if sub == n_sub - 1
                     else _rms(x_ref[r0 + ROW_TILE:r0 + ROW_TILE + HALO, :], g))
        _mlp_rows(x_ref.at[r0:r0 + ROW_TILE], halo_prev, halo_next, g, wup_ref, cw_ref, cb_ref, wdn_ref,
                  gf_ref, out_ref.at[r0:r0 + ROW_TILE], act_ref.at[sub], perm_ref.at[sub])


def _mlp_rows(x_ref, halo_prev, halo_next, g, wup_ref, cw_ref, cb_ref, wdn_ref, gf_ref, out_ref,
              act_ref, perm_ref):
    t = ROW_TILE
    nv = t // HALO
    ncb = D_MODEL // PAIR
    x = x_ref[...]
    h = _rms(x, g)
    for r in range(HALO):
        for c in range(ncb):
            perm_ref[c, pl.ds(r, nv, stride=HALO), :] = h[r * nv:(r + 1) * nv, c * PAIR:(c + 1) * PAIR]
    hp = jnp.concatenate([perm_ref[c] for c in range(ncb)], axis=1)
    sub_d = lax.broadcasted_iota(jnp.int32, (HALO, D_MODEL), 0)
    halo = jnp.where(sub_d == 0, halo_prev[HALO - 1:], jnp.where(sub_d == 1, halo_next[:1], 0.0))
    he = jnp.concatenate([hp, halo], axis=0).astype(BF16)

    def conv(u, c0):
        n = u.shape[1]
        w = cw_ref[:, c0:c0 + n]
        sub = lax.broadcasted_iota(jnp.int32, (HALO, n), 0)
        main, u_halo = u[:t], u[t:]
        head = jnp.where(sub == 0, u_halo[0:1], pltpu.roll(main[t - HALO:], 1, axis=0))
        tail = jnp.where(sub == HALO - 1, u_halo[1:2], pltpu.roll(main[:HALO], HALO - 1, axis=0))
        prev = jnp.concatenate([head, main[:t - HALO]], axis=0)
        nxt = jnp.concatenate([main[HALO:], tail], axis=0)
        return prev * w[0:1] + main * w[1:2] + nxt * w[2:3] + cb_ref[:, c0:c0 + n]

    for c0 in range(0, D_FF, FF_CHUNK):
        c1 = min(c0 + FF_CHUNK, D_FF)
        ug = jnp.dot(he, wup_ref[:, c0:c1], preferred_element_type=F32)
        uv = jnp.dot(he, wup_ref[:, D_FF + c0:D_FF + c1], preferred_element_type=F32)
        act_ref[:, c0:c1] = (_gelu_tanh(conv(ug, c0)) * conv(uv, D_FF + c0)).astype(BF16)
    yp = jnp.dot(act_ref[...], wdn_ref[...], preferred_element_type=F32)
    for c in range(ncb):
        perm_ref[c] = yp[:, c * PAIR:(c + 1) * PAIR]
    y = x + jnp.concatenate(
        [jnp.concatenate([perm_ref[c, pl.ds(r, nv, stride=HALO), :] for c in range(ncb)], axis=1)
         for r in range(HALO)], axis=0)
    out_ref[...] = _rms(y, gf_ref[...])


def _mlp_call(x, s, g, wup, cw, cb, wdn, gf):
    n = x.shape[0]
    t = MLP_STEP_TILES * ROW_TILE
    hb = t // HALO
    nhb = n // HALO
    consts = (g, wup, cw, cb, wdn, gf)
    return pl.pallas_call(
        functools.partial(_mlp_kernel, steps_per_seq=s // t),
        grid=(n // t,),
        in_specs=[pl.BlockSpec((HALO, D_MODEL), lambda i: (jnp.maximum(i * hb - 1, 0), 0)),
                  pl.BlockSpec((t, D_MODEL), lambda i: (i, 0)),
                  pl.BlockSpec((HALO, D_MODEL), lambda i: (jnp.minimum((i + 1) * hb, nhb - 1), 0))]
        + [_const_spec(c.shape) for c in consts],
        out_specs=pl.BlockSpec((t, D_MODEL), lambda i: (i, 0)),
        out_shape=jax.ShapeDtypeStruct((n, D_MODEL), F32),
        scratch_shapes=[pltpu.VMEM((MLP_STEP_TILES, ROW_TILE, D_FF), BF16),
                        pltpu.VMEM((MLP_STEP_TILES, D_MODEL // PAIR, ROW_TILE, PAIR), F32)],
        compiler_params=_params(1),
        name="conv_mlp",
    )(x, x, x, *consts)


def _rel_bucket(rel):
    nb = N_BUCKETS // 2
    max_exact = nb // 2
    rel = np.asarray(rel, np.int32)
    ret = np.where(rel > 0, nb, 0)
    n = np.abs(rel)
    nf = np.maximum(n, 1).astype(np.float32)
    large = max_exact + (np.log(nf / np.float32(max_exact)) / np.float32(math.log(MAX_DISTANCE / max_exact))
                         * np.float32(nb - max_exact)).astype(np.int32)
    large = np.minimum(large, nb - 1)
    return (ret + np.where(n < max_exact, n, large)).astype(np.int32)


def _bias_tiles(rel_bias, head0, n_heads, dilation, win, side, shifts):
    offs = np.arange(-side, side + 1) * dilation
    line = rel_bias[_rel_bucket(offs)][:, head0:head0 + n_heads].T.astype(F32) * LOG2E
    k0 = QBLK + max(shifts) - side
    p = win + k0 + side + 1
    padded = jnp.pad(line, ((0, 0), (k0, p - k0 - (2 * side + 1))), constant_values=NEG)
    skew = jnp.tile(padded, (1, QBLK + 1))[:, :QBLK * (p - 1)].reshape(n_heads, QBLK, p - 1)
    return jnp.stack([skew[:, :, k0 - sh + side:k0 - sh + side + win] for sh in shifts], 0)


def _prep(w_in, b_gate, rel_bias, sink, w_a_out, w_b_out, w_o, w_up, conv_w, conv_b, w_down,
          g_attn, g_ffn, g_final):
    row = lambda v: v.reshape(1, -1).astype(F32)
    p = dict(
        w_in=w_in.astype(BF16), b_gate=row(b_gate),
        w_a_out=w_a_out.astype(BF16), w_b_out=w_b_out.astype(BF16), w_o=w_o.astype(BF16),
        w_up=w_up.astype(BF16), conv_w=conv_w.astype(F32), conv_b=row(conv_b),
        w_down=w_down.astype(BF16), g_attn=row(g_attn), g_ffn=row(g_ffn), g_final=row(g_final),
        rel_bias=rel_bias,
    )
    bb = _bias_tiles(rel_bias, A_HEADS, B_Q_HEADS, 1, B_WIN, B_RADIUS, (0, B_RADIUS, 2 * B_RADIUS))
    p["bias_b"] = _pair_lanes(bb.reshape(3, B_KV_HEADS, 2, 2, QBLK, B_WIN)).transpose(1, 0, 2, 3, 4)
    p["sink"] = sink.astype(F32) * LOG2E
    return p


def _pair_lanes(t):
    return jnp.concatenate([t[..., 0, :, :], t[..., 1, :, :]], axis=-1)


def _group_biases(rel_bias, s):
    out = []
    for gi, d in enumerate(DILATIONS):
        m = s // d
        win = min(A_WIN, m)
        shifts = (0,) if m == QBLK else (0, A_SIDE, 2 * A_SIDE)
        t = _bias_tiles(rel_bias, gi * A_HEADS_PER_GROUP, A_HEADS_PER_GROUP, d, win, A_SIDE, shifts)
        out.append(_pair_lanes(t.reshape(len(shifts), 2, 2, QBLK, win)))
    return out


def _trunk(x, p):
    b, s, _ = x.shape
    xf = x.reshape(b * s, D_MODEL)
    g1, g2, g3, qb = _qkv_call(xf, p["g_attn"], p["w_in"], b, s)
    a = _attn_a_call((g1, g2, g3), _group_biases(p["rel_bias"], s), b, s)
    bo = _attn_b_call(qb, p["bias_b"], p["sink"], b, s)
    x1 = _post_call(xf, a, bo, p["g_attn"], p["w_in"], p["b_gate"], p["w_a_out"], p["w_b_out"], p["w_o"])
    y = _mlp_call(x1, s, p["g_ffn"], p["w_up"], p["conv_w"], p["conv_b"], p["w_down"], p["g_final"])
    return y.reshape(b, s, D_MODEL)


def kernel(x_prompt, x_sample, g_attn, w_in, b_gate, rel_bias, sink, w_a_out, w_b_out, w_o, g_ffn,
           w_up, conv_w, conv_b, w_down, g_final):
    assert w_in.shape[0] == 1, "one layer: the final RMSNorm is fused into its MLP kernel"
    p = _prep(w_in[0], b_gate[0], rel_bias, sink[0], w_a_out[0], w_b_out[0], w_o[0], w_up[0],
              conv_w[0], conv_b[0], w_down[0], g_attn[0], g_ffn[0], g_final)
    return (_trunk(x_prompt, p), _trunk(x_sample, p))
```

```python
import functools
import math

import numpy as np
import jax
import jax.numpy as jnp
from jax import lax
from jax.experimental import pallas as pl
from jax.experimental.pallas import tpu as pltpu

F32 = jnp.float32
BF16 = jnp.bfloat16

D_MODEL = 1024
HEAD_DIM = 64
DIL_PATTERNS = ((128, 1), (512, 4), (2048, 16))
DILATIONS = tuple(d for _, d in DIL_PATTERNS)
A_HEADS_PER_GROUP = 4
A_GROUP_WIDTH = A_HEADS_PER_GROUP * HEAD_DIM
N_GROUPS = len(DIL_PATTERNS)
A_HEADS = N_GROUPS * A_HEADS_PER_GROUP
A_WIDTH = A_HEADS * HEAD_DIM
A_SIDE = 64
B_Q_HEADS = 8
B_KV_HEADS = 2
B_GROUP = B_Q_HEADS // B_KV_HEADS
B_Q_WIDTH = B_Q_HEADS * HEAD_DIM
B_KV_WIDTH = B_KV_HEADS * HEAD_DIM
B_RADIUS = 128
N_BUCKETS = 32
MAX_DISTANCE = 1024
D_FF = 2816
EPS = 1e-6
NEG = -1e30
LOG2E = math.log2(math.e)
Q_SCALE = HEAD_DIM ** -0.5 * LOG2E

KA_OFF, VA_OFF = A_WIDTH, 2 * A_WIDTH
QB_OFF = 3 * A_WIDTH
KVB_OFF = QB_OFF + B_Q_WIDTH
GATE_OFF = KVB_OFF + 2 * B_KV_WIDTH
QKV_COLS = GATE_OFF

PAIR = 2 * HEAD_DIM
PAIR_COLS = 3 * PAIR
GRP_COLS = 2 * PAIR_COLS
B_COLS = B_Q_WIDTH + 4 * PAIR
QBLK = 128
A_WIN = QBLK + 2 * A_SIDE
B_WIN = QBLK + 2 * B_RADIUS
ROW_TILE = 512
STEP_TILES = 2
MLP_STEP_TILES = 2
MXU_COLS = 256
FF_CHUNK = MXU_COLS
HALO = 8
SPLIT_FREE_STRIDE = 4
VMEM_LIMIT = 56 * 1024 * 1024


def _params(n_axes):
    return pltpu.CompilerParams(dimension_semantics=("arbitrary",) * n_axes,
                                vmem_limit_bytes=VMEM_LIMIT)


def _const_spec(shape, index=None):
    index = (0,) * len(shape) if index is None else index
    return pl.BlockSpec(shape, lambda *_: index, pipeline_mode=pl.Buffered(1))


def _rms(x, g):
    ms = jnp.mean(x * x, axis=-1, keepdims=True)
    return x * lax.rsqrt(ms + EPS) * g


def _lane_lo(rows=QBLK):
    return lax.broadcasted_iota(jnp.int32, (rows, PAIR), 1) < HEAD_DIM


def _pair_rhs(v):
    lo = _lane_lo(v.shape[0])
    zero = jnp.zeros_like(v)
    ones_lo = jnp.where(lo, 1.0, 0.0).astype(v.dtype)
    ones_hi = jnp.where(lo, 0.0, 1.0).astype(v.dtype)
    top = jnp.concatenate([jnp.where(lo, v, zero), ones_lo], axis=1)
    bot = jnp.concatenate([jnp.where(lo, zero, v), ones_hi], axis=1)
    return jnp.concatenate([top, bot], axis=0)


def _pair_scores(q, k):
    lo = _lane_lo(k.shape[0])
    zero = jnp.zeros_like(k)
    kbd = jnp.concatenate([jnp.where(lo, k, zero), jnp.where(lo, zero, k)], axis=0)
    return lax.dot_general(q, kbd, (((1,), (1,)), ((), ())), preferred_element_type=F32)


def _pair_softmax(sc, floors=(None, None)):
    win = sc.shape[1] // 2
    out = []
    for half, floor in enumerate(floors):
        s = sc[:, half * win:(half + 1) * win]
        m = jnp.broadcast_to(jnp.max(s, axis=-1, keepdims=True), (sc.shape[0], PAIR))
        if floor is not None:
            m = jnp.maximum(m, floor)
        out.append((m, jnp.exp2(s - jnp.concatenate([m] * (win // PAIR), axis=1))))
    return out


def _qkv_kernel(x_ref, g_ref, w_ref, o1_ref, o2_ref, o3_ref, ob_ref, tmp_ref, tmp2_ref):
    for sub in range(x_ref.shape[0] // ROW_TILE):
        _qkv_rows(sub, x_ref, g_ref, w_ref, (o1_ref, o2_ref, o3_ref), ob_ref, tmp_ref.at[sub],
                  tmp2_ref.at[sub])


def _qkv_rows(sub, x_ref, g_ref, w_ref, o_refs, ob_ref, tmp_ref, tmp2_ref):
    t = ROW_TILE
    rows = slice(sub * t, (sub + 1) * t)
    h = _rms(x_ref[rows, :], g_ref[...]).astype(BF16)

    def proj(c0, n):
        return jnp.dot(h, w_ref[:, c0:c0 + n], preferred_element_type=F32)

    for gi, (o_ref, d) in enumerate(zip(o_refs, DILATIONS)):
        c0 = gi * A_GROUP_WIDTH
        parts = (proj(c0, A_GROUP_WIDTH) * Q_SCALE, proj(KA_OFF + c0, A_GROUP_WIDTH),
                 proj(VA_OFF + c0, A_GROUP_WIDTH))
        for p in range(2):
            for j, part in enumerate(parts):
                c = p * 3 + j
                val = part[:, p * PAIR:(p + 1) * PAIR]
                if d == 1:
                    o_ref[0, rows, c * PAIR:(c + 1) * PAIR] = val.astype(BF16)
                else:
                    tmp_ref[c] = val
        if d > 1:
            td = t // d
            src_ref, d_in = tmp_ref, d
            if d > SPLIT_FREE_STRIDE:
                d_out = SPLIT_FREE_STRIDE
                d_in = d // d_out
                assert d_in <= SPLIT_FREE_STRIDE
                for r1 in range(d_out):
                    for c in range(GRP_COLS // PAIR):
                        tmp2_ref[c, r1 * (t // d_out):(r1 + 1) * (t // d_out), :] = (
                            tmp_ref[c, pl.ds(r1, t // d_out, stride=d_out), :])
                src_ref = tmp2_ref
            for r in range(d):
                base = (r % (d // d_in)) * (t // (d // d_in)) + r // (d // d_in)
                for c in range(GRP_COLS // PAIR):
                    o_ref[r, sub * td:(sub + 1) * td, c * PAIR:(c + 1) * PAIR] = (
                        src_ref[c, pl.ds(base, td, stride=d_in), :].astype(BF16))

    ob_ref[rows, :B_Q_WIDTH] = (proj(QB_OFF, B_Q_WIDTH) * Q_SCALE).astype(BF16)
    kv = proj(KVB_OFF, 2 * B_KV_WIDTH)
    lo = _lane_lo(t)
    for j in range(2):
        one = kv[:, j * PAIR:(j + 1) * PAIR]
        swapped = pltpu.roll(one, HEAD_DIM, axis=1)
        c0 = B_Q_WIDTH + j * 2 * PAIR
        ob_ref[rows, c0:c0 + PAIR] = jnp.where(lo, one, swapped).astype(BF16)
        ob_ref[rows, c0 + PAIR:c0 + 2 * PAIR] = jnp.where(lo, swapped, one).astype(BF16)


def _qkv_call(x, g, w_in, b, s):
    n = x.shape[0]
    t = STEP_TILES * ROW_TILE
    tps = s // t
    grp_shape = lambda d: jax.ShapeDtypeStruct((b, d, s // d, GRP_COLS), BF16)
    grp_spec = lambda d: pl.BlockSpec((None, d, t // d, GRP_COLS), lambda i: (i // tps, 0, i % tps, 0))
    return pl.pallas_call(
        _qkv_kernel,
        grid=(n // t,),
        in_specs=[pl.BlockSpec((t, D_MODEL), lambda i: (i, 0)), _const_spec(g.shape),
                  _const_spec((D_MODEL, QKV_COLS))],
        out_specs=[grp_spec(d) for d in DILATIONS] + [pl.BlockSpec((t, B_COLS), lambda i: (i, 0))],
        out_shape=[grp_shape(d) for d in DILATIONS] + [jax.ShapeDtypeStruct((n, B_COLS), BF16)],
        scratch_shapes=[pltpu.VMEM((STEP_TILES, GRP_COLS // PAIR, ROW_TILE, PAIR), F32)] * 2,
        compiler_params=_params(1),
        name="qkv_proj",
    )(x, g, w_in)


def _window(i, nblk, m, win, side):
    if nblk == 1:
        return 0, 0
    start = pl.multiple_of(jnp.clip(i * QBLK - side, 0, m - win), 64)
    var = jnp.where(i == 0, 0, jnp.where(i == nblk - 1, 2, 1))
    return start, var


def _attn_a_kernel(g1_ref, g2_ref, g3_ref, b1_ref, b2_ref, b3_ref, out_ref, state_ref, *, s):
    lo = _lane_lo()

    def attend(q, k, v, bias):
        (m_e, e_e), (m_o, e_o) = _pair_softmax(_pair_scores(q, k) + bias)
        e = jnp.concatenate([e_e, e_o], axis=1).astype(BF16)
        r = jnp.dot(e, _pair_rhs(v), preferred_element_type=F32)
        return r[:, :PAIR], jnp.where(lo, m_e, m_o), r[:, PAIR:]

    order = sorted(range(N_GROUPS), key=lambda gi: -DILATIONS[gi])
    assert DILATIONS[order[-1]] == 1
    refs, bias_refs = (g1_ref, g2_ref, g3_ref), (b1_ref, b2_ref, b3_ref)
    for step, gi in enumerate(order):
        ref, bias_ref, d = refs[gi], bias_refs[gi], DILATIONS[gi]
        last = step == N_GROUPS - 1
        d_next = 1 if last else DILATIONS[order[step + 1]]
        assert d % d_next == 0 and d // d_next <= SPLIT_FREE_STRIDE
        src_ref, dst_ref = state_ref.at[(step + 1) % 2], state_ref.at[step % 2]
        m = s // d
        nblk = m // QBLK
        win = min(A_WIN, m)
        shift = nblk.bit_length() - 1
        assert nblk == 1 << shift

        def body(j, carry, ref=ref, bias_ref=bias_ref, d=d, m=m, nblk=nblk, win=win, shift=shift, step=step,
                 last=last, d_next=d_next, src_ref=src_ref, dst_ref=dst_ref):
            r, i = (0, j) if d == 1 else (j >> shift, j & (nblk - 1))
            q0 = pl.multiple_of(i * QBLK, QBLK)
            start, var = _window(i, nblk, m, win, A_SIDE)
            o, mm, ll = attend(ref[r, pl.ds(q0, QBLK), :PAIR], ref[r, pl.ds(start, win), PAIR:2 * PAIR],
                               ref[r, pl.ds(start, win), 2 * PAIR:], bias_ref[var])
            if step > 0:
                own = pl.ds(pl.multiple_of(r * m + q0, QBLK), QBLK)
                m_old = src_ref[1, own, :]
                m_new = jnp.maximum(m_old, mm)
                alpha = jnp.exp2(m_old - m_new)
                beta = jnp.exp2(mm - m_new)
                ll = alpha * src_ref[2, own, :] + beta * ll
                o = alpha * src_ref[0, own, :] + beta * o
                mm = m_new
            if last:
                out_ref[pl.ds(q0, QBLK), :] = (o / ll).astype(BF16)
                return carry
            rows = pl.ds((r % d_next) * (s // d_next) + r // d_next + (d // d_next) * q0, QBLK,
                         stride=d // d_next)
            dst_ref[0, rows, :] = o
            dst_ref[1, rows, :] = mm
            dst_ref[2, rows, :] = ll
            return carry

        lax.fori_loop(0, d * nblk, body, 0, unroll=16)


def _attn_a_call(groups, biases, b, s):
    in_specs = [pl.BlockSpec((None, d, s // d, PAIR_COLS), lambda p, bi: (bi, 0, 0, p)) for d in DILATIONS]
    in_specs += [pl.BlockSpec((bias.shape[0], None) + bias.shape[2:], lambda p, bi: (0, p, 0, 0))
                 for bias in biases]
    out = pl.pallas_call(
        functools.partial(_attn_a_kernel, s=s),
        grid=(2, b),
        in_specs=in_specs,
        out_specs=pl.BlockSpec((None, s, PAIR), lambda p, bi: (bi, 0, p)),
        out_shape=jax.ShapeDtypeStruct((b, s, A_GROUP_WIDTH), BF16),
        scratch_shapes=[pltpu.VMEM((2, 3, s, PAIR), F32)],
        compiler_params=_params(2),
        name="attn_dil",
    )(*groups, *biases)
    return out.reshape(b * s, A_GROUP_WIDTH)


def _attn_b_kernel(q_ref, k_ref, v_ref, bias_ref, sink_ref, o_ref, *, s):
    nblk = s // QBLK
    lo = _lane_lo()
    c = pl.program_id(0)

    def block(i, carry):
        q0 = pl.multiple_of(i * QBLK, QBLK)
        start, var = _window(i, nblk, s, B_WIN, B_RADIUS)
        q = jnp.concatenate([q_ref[pl.ds(q0, QBLK), :PAIR], q_ref[pl.ds(q0, QBLK), PAIR:]], axis=0)
        sc = _pair_scores(q, k_ref[pl.ds(start, B_WIN), :])
        rhs = _pair_rhs(v_ref[pl.ds(start, B_WIN), :])
        es, sinks = [], []
        for p in range(2):
            sk_e = sink_ref[c * B_GROUP + 2 * p]
            sk_o = sink_ref[c * B_GROUP + 2 * p + 1]
            (m_e, e_e), (m_o, e_o) = _pair_softmax(sc[p * QBLK:(p + 1) * QBLK] + bias_ref[var, p], (sk_e, sk_o))
            es.append(jnp.concatenate([e_e, e_o], axis=1).astype(BF16))
            sinks.append(jnp.where(lo, jnp.exp2(sk_e - m_e), jnp.exp2(sk_o - m_o)))
        r = jnp.dot(jnp.concatenate(es, axis=0), rhs, preferred_element_type=F32)
        for p in range(2):
            rp = r[p * QBLK:(p + 1) * QBLK]
            o_ref[pl.ds(q0, QBLK), p * PAIR:(p + 1) * PAIR] = (rp[:, :PAIR] / (rp[:, PAIR:] + sinks[p])).astype(BF16)
        return carry

    lax.fori_loop(0, nblk, block, 0, unroll=16)


def _attn_b_call(qkv_b, bias, sink, b, s):
    qkv_v = qkv_b.reshape(b, s, B_COLS)
    qw = B_GROUP * HEAD_DIM
    kv0 = B_Q_WIDTH // PAIR
    o = pl.pallas_call(
        functools.partial(_attn_b_kernel, s=s),
        grid=(B_KV_HEADS, b),
        in_specs=[pl.BlockSpec((None, s, qw), lambda c, bi: (bi, 0, c)),
                  pl.BlockSpec((None, s, PAIR), lambda c, bi: (bi, 0, kv0 + c)),
                  pl.BlockSpec((None, s, PAIR), lambda c, bi: (bi, 0, kv0 + B_KV_HEADS + c)),
                  pl.BlockSpec((None,) + bias.shape[1:], lambda c, bi: (c, 0, 0, 0, 0)),
                  pl.BlockSpec(memory_space=pltpu.SMEM)],
        out_specs=pl.BlockSpec((None, s, qw), lambda c, bi: (bi, 0, c)),
        out_shape=jax.ShapeDtypeStruct((b, s, B_Q_WIDTH), BF16),
        compiler_params=_params(2),
        name="attn_win",
    )(qkv_v, qkv_v, qkv_v, bias, sink)
    return o.reshape(b * s, B_Q_WIDTH)


def _post_kernel(x_ref, a_ref, bo_ref, g_ref, wga_ref, wgb_ref, bg_ref, wa_ref, wb_ref, wo_ref,
                 out_ref, mix_ref):
    for sub in range(x_ref.shape[0] // ROW_TILE):
        rows = slice(sub * ROW_TILE, (sub + 1) * ROW_TILE)
        x = x_ref[rows, :]
        h = _rms(x, g_ref[...]).astype(BF16)
        a = a_ref[rows, :]
        bo = bo_ref[rows, :]
        for c0 in range(0, D_MODEL, MXU_COLS):
            cs = slice(c0, c0 + MXU_COLS)
            ap = jnp.dot(a, wa_ref[:, cs], preferred_element_type=F32)
            bp = jnp.dot(bo, wb_ref[:, cs], preferred_element_type=F32)
            ga = jnp.dot(h, wga_ref[:, cs], preferred_element_type=F32) + bg_ref[:, cs]
            gb = (jnp.dot(h, wgb_ref[:, cs], preferred_element_type=F32)
                  + bg_ref[:, D_MODEL + c0:D_MODEL + c0 + MXU_COLS])
            mix_ref[rows, cs] = (jax.nn.sigmoid(ga) * ap + jax.nn.sigmoid(gb) * bp).astype(BF16)
        out_ref[rows, :] = x + jnp.dot(mix_ref[rows, :], wo_ref[...], preferred_element_type=F32)


def _post_call(x, a, bo, g, w_in, bg, wa, wb, wo):
    n = x.shape[0]
    t = STEP_TILES * ROW_TILE
    rows = lambda w: pl.BlockSpec((t, w), lambda i: (i, 0))
    gate_blk = GATE_OFF // D_MODEL
    return pl.pallas_call(
        _post_kernel,
        grid=(n // t,),
        in_specs=[rows(D_MODEL), rows(A_GROUP_WIDTH), rows(B_Q_WIDTH), _const_spec(g.shape),
                  _const_spec((D_MODEL, D_MODEL), (0, gate_blk)),
                  _const_spec((D_MODEL, D_MODEL), (0, gate_blk + 1)),
                  _const_spec(bg.shape), _const_spec(wa.shape), _const_spec(wb.shape), _const_spec(wo.shape)],
        out_specs=rows(D_MODEL),
        out_shape=jax.ShapeDtypeStruct((n, D_MODEL), F32),
        scratch_shapes=[pltpu.VMEM((t, D_MODEL), BF16)],
        compiler_params=_params(1),
        name="post_attn",
    )(x, a, bo, g, w_in, w_in, bg, wa, wb, wo)


def _gelu_tanh(x):
    k = -2.0 * LOG2E * math.sqrt(2.0 / math.pi)
    return x / (1.0 + jnp.exp2(x * (k + (k * 0.044715) * (x * x))))


def _mlp_kernel(xp_ref, x_ref, xn_ref, g_ref, wup_ref, cw_ref, cb_ref, wdn_ref, gf_ref, out_ref,
                act_ref, perm_ref, *, steps_per_seq):
    n_sub = x_ref.shape[0] // ROW_TILE
    i = pl.program_id(0)
    keep_prev = jnp.where((i % steps_per_seq) == 0, 0.0, 1.0)
    keep_next = jnp.where((i % steps_per_seq) == steps_per_seq - 1, 0.0, 1.0)
    g = g_ref[...]
    for sub in range(n_sub):
        r0 = sub * ROW_TILE
        halo_prev = (_rms(xp_ref[...], g) * keep_prev if sub == 0
                     else _rms(x_ref[r0 - HALO:r0, :], g))
        halo_next = (_rms(xn_ref[...], g) * keep_next if sub == n_sub - 1
                     else _rms(x_ref[r0 + ROW_TILE:r0 + ROW_TILE + HALO, :], g))
        _mlp_rows(x_ref.at[r0:r0 + ROW_TILE], halo_prev, halo_next, g, wup_ref, cw_ref, cb_ref, wdn_ref,
                  gf_ref, out_ref.at[r0:r0 + ROW_TILE], act_ref.at[sub], perm_ref.at[sub])


def _mlp_rows(x_ref, halo_prev, halo_next, g, wup_ref, cw_ref, cb_ref, wdn_ref, gf_ref, out_ref,
              act_ref, perm_ref):
    t = ROW_TILE
    nv = t // HALO
    ncb = D_MODEL // PAIR
    for r in range(HALO):
        h = _rms(x_ref[r * nv:(r + 1) * nv, :], g)
        for c in range(ncb):
            perm_ref[c, pl.ds(r, nv, stride=HALO), :] = h[:, c * PAIR:(c + 1) * PAIR]
    hp = jnp.concatenate([perm_ref[c] for c in range(ncb)], axis=1)
    sub_d = lax.broadcasted_iota(jnp.int32, (HALO, D_MODEL), 0)
    halo = jnp.where(sub_d == 0, halo_prev[HALO - 1:], jnp.where(sub_d == 1, halo_next[:1], 0.0))
    he = jnp.concatenate([hp, halo], axis=0).astype(BF16)

    def conv(u, c0):
        n = u.shape[1]
        w = cw_ref[:, c0:c0 + n]
        sub = lax.broadcasted_iota(jnp.int32, (HALO, n), 0)
        main, u_halo = u[:t], u[t:]
        head = jnp.where(sub == 0, u_halo[0:1], pltpu.roll(main[t - HALO:], 1, axis=0))
        tail = jnp.where(sub == HALO - 1, u_halo[1:2], pltpu.roll(main[:HALO], HALO - 1, axis=0))
        prev = jnp.concatenate([head, main[:t - HALO]], axis=0)
        nxt = jnp.concatenate([main[HALO:], tail], axis=0)
        return prev * w[0:1] + main * w[1:2] + nxt * w[2:3] + cb_ref[:, c0:c0 + n]

    for c0 in range(0, D_FF, FF_CHUNK):
        c1 = min(c0 + FF_CHUNK, D_FF)
        ug = jnp.dot(he, wup_ref[:, c0:c1], preferred_element_type=F32)
        uv = jnp.dot(he, wup_ref[:, D_FF + c0:D_FF + c1], preferred_element_type=F32)
        act_ref[:, c0:c1] = (_gelu_tanh(conv(ug, c0)) * conv(uv, D_FF + c0)).astype(BF16)
    yp = jnp.dot(act_ref[...], wdn_ref[...], preferred_element_type=F32)
    for c in range(ncb):
        perm_ref[c] = yp[:, c * PAIR:(c + 1) * PAIR]
    for r in range(HALO):
        seg = slice(r * nv, (r + 1) * nv)
        y = x_ref[seg, :] + jnp.concatenate(
            [perm_ref[c, pl.ds(r, nv, stride=HALO), :] for c in range(ncb)], axis=1)
        out_ref[seg, :] = _rms(y, gf_ref[...])


def _mlp_call(x, s, g, wup, cw, cb, wdn, gf):
    n = x.shape[0]
    t = MLP_STEP_TILES * ROW_TILE
    hb = t // HALO
    nhb = n // HALO
    consts = (g, wup, cw, cb, wdn, gf)
    return pl.pallas_call(
        functools.partial(_mlp_kernel, steps_per_seq=s // t),
        grid=(n // t,),
        in_specs=[pl.BlockSpec((HALO, D_MODEL), lambda i: (jnp.maximum(i * hb - 1, 0), 0)),
                  pl.BlockSpec((t, D_MODEL), lambda i: (i, 0)),
                  pl.BlockSpec((HALO, D_MODEL), lambda i: (jnp.minimum((i + 1) * hb, nhb - 1), 0))]
        + [_const_spec(c.shape) for c in consts],
        out_specs=pl.BlockSpec((t, D_MODEL), lambda i: (i, 0)),
        out_shape=jax.ShapeDtypeStruct((n, D_MODEL), F32),
        scratch_shapes=[pltpu.VMEM((MLP_STEP_TILES, ROW_TILE, D_FF), BF16),
                        pltpu.VMEM((MLP_STEP_TILES, D_MODEL // PAIR, ROW_TILE, PAIR), F32)],
        compiler_params=_params(1),
        name="conv_mlp",
    )(x, x, x, *consts)


def _rel_bucket(rel):
    nb = N_BUCKETS // 2
    max_exact = nb // 2
    rel = np.asarray(rel, np.int32)
    ret = np.where(rel > 0, nb, 0)
    n = np.abs(rel)
    nf = np.maximum(n, 1).astype(np.float32)
    large = max_exact + (np.log(nf / np.float32(max_exact)) / np.float32(math.log(MAX_DISTANCE / max_exact))
                         * np.float32(nb - max_exact)).astype(np.int32)
    large = np.minimum(large, nb - 1)
    return (ret + np.where(n < max_exact, n, large)).astype(np.int32)


def _bias_tiles(rel_bias, head0, n_heads, dilation, win, side, shifts):
    offs = np.arange(-side, side + 1) * dilation
    line = rel_bias[_rel_bucket(offs)][:, head0:head0 + n_heads].T.astype(F32) * LOG2E
    k0 = QBLK + max(shifts) - side
    p = win + k0 + side + 1
    padded = jnp.pad(line, ((0, 0), (k0, p - k0 - (2 * side + 1))), constant_values=NEG)
    skew = jnp.tile(padded, (1, QBLK + 1))[:, :QBLK * (p - 1)].reshape(n_heads, QBLK, p - 1)
    return jnp.stack([skew[:, :, k0 - sh + side:k0 - sh + side + win] for sh in shifts], 0)


def _prep(w_in, b_gate, rel_bias, sink, w_a_out, w_b_out, w_o, w_up, conv_w, conv_b, w_down,
          g_attn, g_ffn, g_final):
    row = lambda v: v.reshape(1, -1).astype(F32)
    p = dict(
        w_in=w_in.astype(BF16), b_gate=row(b_gate),
        w_a_out=w_a_out.astype(BF16), w_b_out=w_b_out.astype(BF16), w_o=w_o.astype(BF16),
        w_up=w_up.astype(BF16), conv_w=conv_w.astype(F32), conv_b=row(conv_b),
        w_down=w_down.astype(BF16), g_attn=row(g_attn), g_ffn=row(g_ffn), g_final=row(g_final),
        rel_bias=rel_bias,
    )
    bb = _bias_tiles(rel_bias, A_HEADS, B_Q_HEADS, 1, B_WIN, B_RADIUS, (0, B_RADIUS, 2 * B_RADIUS))
    p["bias_b"] = _pair_lanes(bb.reshape(3, B_KV_HEADS, 2, 2, QBLK, B_WIN)).transpose(1, 0, 2, 3, 4)
    p["sink"] = sink.astype(F32) * LOG2E
    return p


def _pair_lanes(t):
    return jnp.concatenate([t[..., 0, :, :], t[..., 1, :, :]], axis=-1)


def _group_biases(rel_bias, s):
    out = []
    for gi, d in enumerate(DILATIONS):
        m = s // d
        win = min(A_WIN, m)
        shifts = (0,) if m == QBLK else (0, A_SIDE, 2 * A_SIDE)
        t = _bias_tiles(rel_bias, gi * A_HEADS_PER_GROUP, A_HEADS_PER_GROUP, d, win, A_SIDE, shifts)
        out.append(_pair_lanes(t.reshape(len(shifts), 2, 2, QBLK, win)))
    return out


def _trunk(x, p):
    b, s, _ = x.shape
    xf = x.reshape(b * s, D_MODEL)
    g1, g2, g3, qb = _qkv_call(xf, p["g_attn"], p["w_in"], b, s)
    a = _attn_a_call((g1, g2, g3), _group_biases(p["rel_bias"], s), b, s)
    bo = _attn_b_call(qb, p["bias_b"], p["sink"], b, s)
    x1 = _post_call(xf, a, bo, p["g_attn"], p["w_in"], p["b_gate"], p["w_a_out"], p["w_b_out"], p["w_o"])
    y = _mlp_call(x1, s, p["g_ffn"], p["w_up"], p["conv_w"], p["conv_b"], p["w_down"], p["g_final"])
    return y.reshape(b, s, D_MODEL)


def kernel(x_prompt, x_sample, g_attn, w_in, b_gate, rel_bias, sink, w_a_out, w_b_out, w_o, g_ffn,
           w_up, conv_w, conv_b, w_down, g_final):
    assert w_in.shape[0] == 1, "one layer: the final RMSNorm is fused into its MLP kernel"
    p = _prep(w_in[0], b_gate[0], rel_bias, sink[0], w_a_out[0], w_b_out[0], w_o[0], w_up[0],
              conv_w[0], conv_b[0], w_down[0], g_attn[0], g_ffn[0], g_final)
    return (_trunk(x_prompt, p), _trunk(x_sample, p))
```
